```python
import math
import jax, jax.numpy as jnp
from jax import lax
import numpy as np

D_MODEL = 2048
BATCH = 4
SEQ = 4096
DEPTH = 4

N_MIXERS = 3
HEAD_DIM = 64
ROPE_DIM = HEAD_DIM // 4
ROPE_THETA = 500000.0
NORM_EPS = 1e-5

A_HEADS = D_MODEL // HEAD_DIM
A_KV_HEADS = A_HEADS // 8
A_GROUP = A_HEADS // A_KV_HEADS
A_WIDTH = A_HEADS * HEAD_DIM
A_KV_WIDTH = A_KV_HEADS * HEAD_DIM
A_IN = 2 * A_WIDTH + 2 * A_KV_WIDTH
A_WINDOW = 128
A_QBLOCK = 128

B_HEADS = D_MODEL // HEAD_DIM
B_KV_GROUPS = 4
B_GROUP = B_HEADS // B_KV_GROUPS
B_WIDTH = B_HEADS * HEAD_DIM
B_KV_WIDTH = B_KV_GROUPS * HEAD_DIM
B_N_BRANCH = 3
B_IN = 2 * B_WIDTH + 6 * B_KV_WIDTH + B_N_BRANCH * B_HEADS
CMP_BLOCK = 32
CMP_STRIDE = 16
CMP_HIDDEN = 4 * HEAD_DIM
SLC_BLOCK = 64
SLC_TOPK = 16
B_WINDOW = 512
B_QBLOCK = 64
WIN_QBLOCK = 128

C_WIDTH = D_MODEL
C_IN = 4 * C_WIDTH
CONV_WIDTH = 3

N_A = (DEPTH + 2) // N_MIXERS
N_B = (DEPTH + 1) // N_MIXERS
N_C = DEPTH // N_MIXERS

kernel_name = "hybrid_swa_nsa_shortconv_trunk"


def rmsnorm(x, g):
    xf = x.astype(jnp.float32)
    y = xf * lax.rsqrt(jnp.mean(xf * xf, axis=-1, keepdims=True) + NORM_EPS)
    return (y * g.astype(jnp.float32)).astype(x.dtype)


def rope_tables(positions):
    inv_freq = ROPE_THETA ** (-jnp.arange(0, ROPE_DIM, 2, dtype=jnp.float32) / ROPE_DIM)
    ang = positions.astype(jnp.float32)[..., None] * inv_freq
    return jnp.cos(ang)[:, :, None, :], jnp.sin(ang)[:, :, None, :]


def apply_rope(x, cos, sin):
    half = ROPE_DIM // 2
    x1 = x[..., :half].astype(jnp.float32)
    x2 = x[..., half:ROPE_DIM].astype(jnp.float32)
    r1 = x1 * cos - x2 * sin
    r2 = x2 * cos + x1 * sin
    return jnp.concatenate([r1.astype(x.dtype), r2.astype(x.dtype), x[..., ROPE_DIM:]], axis=-1)


def banded_attention(q, k, v, window, qblock, sinks=None):
    Bn, S, G, Hg, hd = q.shape
    nqb = S // qblock
    span = window + qblock
    scale = hd ** -0.5
    k_pad = jnp.pad(k, ((0, 0), (window, 0), (0, 0), (0, 0)))
    v_pad = jnp.pad(v, ((0, 0), (window, 0), (0, 0), (0, 0)))
    q_blocks = q.reshape(Bn, nqb, qblock, G, Hg, hd).swapaxes(0, 1)
    starts = jnp.arange(nqb, dtype=jnp.int32) * qblock
    rel = jnp.arange(qblock)[:, None] - jnp.arange(span)[None, :] + window
    band = (rel >= 0) & (rel < window)

    def step(args):
        qb, s0 = args
        kb = lax.dynamic_slice_in_dim(k_pad, s0, span, axis=1).astype(jnp.float32)
        vb = lax.dynamic_slice_in_dim(v_pad, s0, span, axis=1).astype(jnp.float32)
        s = jnp.einsum('bqghd,bkgd->bghqk', qb.astype(jnp.float32), kb) * scale
        valid = band & ((s0 - window + jnp.arange(span)) >= 0)[None, :]
        s = jnp.where(valid, s, -jnp.inf)
        if sinks is None:
            p = jax.nn.softmax(s, axis=-1)
        else:
            sk = sinks.astype(jnp.float32)[None, :, :, None, None]
            m = jnp.maximum(jnp.max(s, axis=-1, keepdims=True), sk)
            e = jnp.exp(s - m)
            p = e / (jnp.sum(e, axis=-1, keepdims=True) + jnp.exp(sk - m))
        o = jnp.einsum('bghqk,bkgd->bqghd', p, vb)
        return o.astype(q.dtype)

    o = lax.map(step, (q_blocks, starts))
    return o.swapaxes(0, 1).reshape(Bn, S, G, Hg, hd)


def mixer_a(h, w_in, sinks, w_out, cos, sin):
    Bn, S, _ = h.shape
    proj = h @ w_in
    q, k, v, z = jnp.split(proj, [A_WIDTH, A_WIDTH + A_KV_WIDTH, A_WIDTH + 2 * A_KV_WIDTH], axis=-1)
    q = apply_rope(q.reshape(Bn, S, A_HEADS, HEAD_DIM), cos, sin).reshape(Bn, S, A_KV_HEADS, A_GROUP, HEAD_DIM)
    k = apply_rope(k.reshape(Bn, S, A_KV_HEADS, HEAD_DIM), cos, sin)
    v = v.reshape(Bn, S, A_KV_HEADS, HEAD_DIM)
    o = banded_attention(q, k, v, A_WINDOW, A_QBLOCK, sinks.reshape(A_KV_HEADS, A_GROUP))
    o = o.reshape(Bn, S, A_WIDTH)
    return (o * jax.nn.silu(z)) @ w_out


def compress(k, pos, w1, w2):
    Bn, S, G, hd = k.shape
    nc = (S - CMP_BLOCK) // CMP_STRIDE + 1
    idx = np.arange(nc)[:, None] * CMP_STRIDE + np.arange(CMP_BLOCK)[None, :]
    blocks = k[:, idx] + pos[None, None, :, None, :]
    blocks = jnp.moveaxis(blocks, 3, 2).reshape(Bn, nc, G, CMP_BLOCK * hd)
    return jax.nn.gelu(blocks @ w1) @ w2


def nsa_cmp_slc(q, kcmp, vcmp, ks, vs):
    Bn, S, G, Hg, hd = q.shape
    nc = kcmp.shape[1]
    nsel = S // SLC_BLOCK
    n_top = min(SLC_TOPK, nsel)
    nqb = S // B_QBLOCK
    scale = hd ** -0.5
    c_start = np.arange(nc) * CMP_STRIDE
    c_end = c_start + CMP_BLOCK
    s_start = np.arange(nsel) * SLC_BLOCK
    s_end = s_start + SLC_BLOCK
    overlap = np.clip(np.minimum(c_end[:, None], s_end[None, :]) - np.maximum(c_start[:, None], s_start[None, :]), 0, None)
    overlap = jnp.asarray(overlap / CMP_BLOCK, jnp.float32)
    c_last = jnp.asarray(c_end - 1, jnp.int32)
    kcmp_f = kcmp.astype(jnp.float32)
    vcmp_f = vcmp.astype(jnp.float32)
    ks_blk = ks.reshape(Bn, nsel, SLC_BLOCK, G, hd).transpose(0, 1, 3, 2, 4)
    vs_blk = vs.reshape(Bn, nsel, SLC_BLOCK, G, hd).transpose(0, 1, 3, 2, 4)
    q_blocks = q.reshape(Bn, nqb, B_QBLOCK, G, Hg, hd).swapaxes(0, 1)
    starts = jnp.arange(nqb, dtype=jnp.int32) * B_QBLOCK
    bi = jnp.arange(Bn)[:, None, None, None]
    gi = jnp.arange(G)[None, None, :, None]
    blk = jnp.arange(nsel)

    def step(args):
        qb, s0 = args
        t = s0 + jnp.arange(B_QBLOCK)
        qf = qb.astype(jnp.float32)
        sc = jnp.einsum('bqghd,bcgd->bghqc', qf, kcmp_f) * scale
        sc = jnp.where(c_last[None, :] <= t[:, None], sc, -jnp.inf)
        m = jnp.max(sc, axis=-1, keepdims=True)
        m = jnp.where(jnp.isfinite(m), m, 0.0)
        e = jnp.exp(sc - m)
        p_cmp = e / jnp.maximum(jnp.sum(e, axis=-1, keepdims=True), 1e-30)
        o_cmp = jnp.einsum('bghqc,bcgd->bqghd', p_cmp, vcmp_f)
        imp = jnp.einsum('bghqc,cn->bgqn', p_cmp, overlap)
        cur = t // SLC_BLOCK
        forced = (blk[None, :] == 0) | (blk[None, :] == cur[:, None]) | (blk[None, :] == cur[:, None] - 1)
        future = blk[None, :] * SLC_BLOCK > t[:, None]
        imp = jnp.where(forced, jnp.inf, imp)
        imp = jnp.where(future, -jnp.inf, imp)
        _, sel = lax.top_k(imp, n_top)
        sel = sel.transpose(0, 2, 1, 3)
        kg = ks_blk[bi, sel, gi].astype(jnp.float32)
        vg = vs_blk[bi, sel, gi].astype(jnp.float32)
        ss = jnp.einsum('bqghd,bqgnld->bghqnl', qf, kg) * scale
        kpos = sel[..., None] * SLC_BLOCK + jnp.arange(SLC_BLOCK)
        valid = (kpos <= t[None, :, None, None, None]).transpose(0, 2, 1, 3, 4)[:, :, None]
        ss = jnp.where(valid, ss, -jnp.inf)
        shp = ss.shape
        p_slc = jax.nn.softmax(ss.reshape(shp[:4] + (n_top * SLC_BLOCK,)), axis=-1).reshape(shp)
        o_slc = jnp.einsum('bghqnl,bqgnld->bqghd', p_slc, vg)
        return o_cmp.astype(q.dtype), o_slc.astype(q.dtype)

    o_cmp, o_slc = lax.map(step, (q_blocks, starts))
    o_cmp = o_cmp.swapaxes(0, 1).reshape(Bn, S, G, Hg, hd)
    o_slc = o_slc.swapaxes(0, 1).reshape(Bn, S, G, Hg, hd)
    return o_cmp, o_slc


def mixer_b(h, w_in, kc_pos, kc_w1, kc_w2, vc_pos, vc_w1, vc_w2, w_out, cos, sin):
    Bn, S, _ = h.shape
    proj = h @ w_in
    cuts = [B_WIDTH + i * B_KV_WIDTH for i in range(7)] + [B_WIDTH + 6 * B_KV_WIDTH + B_N_BRANCH * B_HEADS]
    q, kc, vc, ks, vs, kw, vw, gates, z = jnp.split(proj, cuts, axis=-1)
    q = apply_rope(q.reshape(Bn, S, B_HEADS, HEAD_DIM), cos, sin).reshape(Bn, S, B_KV_GROUPS, B_GROUP, HEAD_DIM)
    kv_shape = (Bn, S, B_KV_GROUPS, HEAD_DIM)
    ks = apply_rope(ks.reshape(kv_shape), cos, sin)
    kw = apply_rope(kw.reshape(kv_shape), cos, sin)
    kcmp = compress(kc.reshape(kv_shape), kc_pos, kc_w1, kc_w2)
    vcmp = compress(vc.reshape(kv_shape), vc_pos, vc_w1, vc_w2)
    o_cmp, o_slc = nsa_cmp_slc(q, kcmp, vcmp, ks, vs.reshape(kv_shape))
    o_win = banded_attention(q, kw, vw.reshape(kv_shape), B_WINDOW, WIN_QBLOCK)
    g = jax.nn.sigmoid(gates.astype(jnp.float32)).reshape(Bn, S, B_N_BRANCH, B_KV_GROUPS, B_GROUP, 1)
    o = (g[:, :, 0] * o_cmp.astype(jnp.float32) + g[:, :, 1] * o_slc.astype(jnp.float32)
         + g[:, :, 2] * o_win.astype(jnp.float32)).astype(h.dtype)
    o = o.reshape(Bn, S, B_WIDTH)
    return (o * jax.nn.silu(z)) @ w_out


def mixer_c(h, w_in, conv_w, w_out):
    u, b_gate, c_gate, z = jnp.split(h @ w_in, 4, axis=-1)
    v = c_gate * u
    y = lax.conv_general_dilated(v, conv_w[:, None, :], window_strides=(1,), padding=[(CONV_WIDTH - 1, 0)],
                                 dimension_numbers=('NWC', 'WIO', 'NWC'), feature_group_count=C_WIDTH)
    y = b_gate * y
    return (y * jax.nn.silu(z)) @ w_out


def setup_inputs(seed: int = 0) -> dict:
    key = jax.random.key(seed)
    ks = jax.random.split(key, 20)

    def dense(k, shape, fan_in):
        return jax.random.normal(k, shape, jnp.float32) * fan_in ** -0.5

    x = jax.random.normal(ks[0], (BATCH, SEQ, D_MODEL), jnp.float32)
    positions = (jnp.arange(SEQ, dtype=jnp.int32)[None, :]
                 + jax.random.randint(ks[1], (BATCH, 1), 0, 1024, dtype=jnp.int32))
    norm_w = 1.0 + 0.02 * jax.random.normal(ks[2], (DEPTH, D_MODEL), jnp.float32)
    final_norm_w = 1.0 + 0.02 * jax.random.normal(ks[3], (D_MODEL,), jnp.float32)
    a_w_in = dense(ks[4], (N_A, D_MODEL, A_IN), D_MODEL)
    a_sinks = jax.random.normal(ks[5], (N_A, A_HEADS), jnp.float32)
    a_w_out = dense(ks[6], (N_A, A_WIDTH, D_MODEL), A_WIDTH)
    b_w_in = dense(ks[7], (N_B, D_MODEL, B_IN), D_MODEL)
    b_cmp_k_pos = 0.1 * jax.random.normal(ks[8], (N_B, CMP_BLOCK, HEAD_DIM), jnp.float32)
    b_cmp_k_w1 = dense(ks[9], (N_B, CMP_BLOCK * HEAD_DIM, CMP_HIDDEN), CMP_BLOCK * HEAD_DIM)
    b_cmp_k_w2 = dense(ks[10], (N_B, CMP_HIDDEN, HEAD_DIM), CMP_HIDDEN)
    b_cmp_v_pos = 0.1 * jax.random.normal(ks[11], (N_B, CMP_BLOCK, HEAD_DIM), jnp.float32)
    b_cmp_v_w1 = dense(ks[12], (N_B, CMP_BLOCK * HEAD_DIM, CMP_HIDDEN), CMP_BLOCK * HEAD_DIM)
    b_cmp_v_w2 = dense(ks[13], (N_B, CMP_HIDDEN, HEAD_DIM), CMP_HIDDEN)
    b_w_out = dense(ks[14], (N_B, B_WIDTH, D_MODEL), B_WIDTH)
    c_w_in = dense(ks[15], (N_C, D_MODEL, C_IN), D_MODEL)
    c_conv_w = dense(ks[16], (N_C, CONV_WIDTH, C_WIDTH), CONV_WIDTH)
    c_w_out = dense(ks[17], (N_C, C_WIDTH, D_MODEL), C_WIDTH)
    return {"x": x, "positions": positions, "norm_w": norm_w, "final_norm_w": final_norm_w,
            "a_w_in": a_w_in, "a_sinks": a_sinks, "a_w_out": a_w_out,
            "b_w_in": b_w_in, "b_cmp_k_pos": b_cmp_k_pos, "b_cmp_k_w1": b_cmp_k_w1, "b_cmp_k_w2": b_cmp_k_w2,
            "b_cmp_v_pos": b_cmp_v_pos, "b_cmp_v_w1": b_cmp_v_w1, "b_cmp_v_w2": b_cmp_v_w2, "b_w_out": b_w_out,
            "c_w_in": c_w_in, "c_conv_w": c_conv_w, "c_w_out": c_w_out}


def reference(x, positions, norm_w, final_norm_w, a_w_in, a_sinks, a_w_out,
              b_w_in, b_cmp_k_pos, b_cmp_k_w1, b_cmp_k_w2, b_cmp_v_pos, b_cmp_v_w1, b_cmp_v_w2, b_w_out,
              c_w_in, c_conv_w, c_w_out):
    cos, sin = rope_tables(positions)
    for i in range(DEPTH):
        h = rmsnorm(x, norm_w[i])
        kind, j = i % N_MIXERS, i // N_MIXERS
        if kind == 0:
            out = mixer_a(h, a_w_in[j], a_sinks[j], a_w_out[j], cos, sin)
        elif kind == 1:
            out = mixer_b(h, b_w_in[j], b_cmp_k_pos[j], b_cmp_k_w1[j], b_cmp_k_w2[j],
                          b_cmp_v_pos[j], b_cmp_v_w1[j], b_cmp_v_w2[j], b_w_out[j], cos, sin)
        else:
            out = mixer_c(h, c_w_in[j], c_conv_w[j], c_w_out[j])
        x = x + out.astype(x.dtype)
    return rmsnorm(x, final_norm_w)
```

```python
import functools
import math

import numpy as np
import jax
import jax.numpy as jnp
from jax import lax
from jax.experimental import pallas as pl
from jax.experimental.pallas import tpu as pltpu

F32 = jnp.float32
BF16 = jnp.bfloat16

D_MODEL = 2048
HEAD_DIM = 64
N_HEADS = 32
N_GROUPS = 4
GROUP = 8
ROPE_DIM = 16
ROPE_HALF = 8
ROPE_THETA = 500000.0
NORM_EPS = 1e-5
A_WINDOW = 128
B_WINDOW = 512
CMP_BLOCK = 32
CMP_STRIDE = 16
SLC_BLOCK = 64
SLC_TOPK = 16
CMP_HIDDEN = 256

LANES = 128
NEG = -2.0 ** 100
VMEM_LIMIT = 56 * 1024 * 1024

PROJ_TM = 1024
PROJ_TN = 512
OUT_TM = 256
ATT_QB = 128
ATT_KB = 256


def _params(sem):
    return pltpu.CompilerParams(dimension_semantics=sem, vmem_limit_bytes=VMEM_LIMIT)


def _norm_proj_kernel(x_ref, g_ref, w_ref, cos_ref, sa_ref, sb_ref, o_ref, h_ref, *,
                      rope_full_tiles, rope_half_tiles):
    j = pl.program_id(1)

    @pl.when(j == 0)
    def _():
        x = x_ref[...]
        ms = jnp.mean(x * x, axis=-1, keepdims=True)
        h_ref[...] = (x * lax.rsqrt(ms + NORM_EPS) * g_ref[...]).astype(BF16)

    acc = jnp.dot(h_ref[...], w_ref[...], preferred_element_type=F32)
    tn = acc.shape[1]

    def rope(a):
        n = a.shape[1]
        reps = n // LANES
        cos = jnp.concatenate([cos_ref[...]] * reps, axis=1)
        sa = jnp.concatenate([sa_ref[...]] * reps, axis=1)
        sb = jnp.concatenate([sb_ref[...]] * reps, axis=1)
        return a * cos + pltpu.roll(a, n - ROPE_HALF, 1) * sa + pltpu.roll(a, ROPE_HALF, 1) * sb

    def any_of(tiles):
        c = j == tiles[0]
        for t in tiles[1:]:
            c = c | (j == t)
        return c

    plain = None
    if rope_full_tiles:
        full = any_of(rope_full_tiles)
        plain = ~full

        @pl.when(full)
        def _():
            o_ref[...] = rope(acc).astype(o_ref.dtype)

    if rope_half_tiles:
        half = any_of(rope_half_tiles)
        plain = ~half if plain is None else plain & ~half

        @pl.when(half)
        def _():
            hw = tn // 2
            o_ref[:, :hw] = rope(acc[:, :hw]).astype(o_ref.dtype)
            o_ref[:, hw:] = acc[:, hw:].astype(o_ref.dtype)

    if plain is None:
        o_ref[...] = acc.astype(o_ref.dtype)
    else:
        @pl.when(plain)
        def _():
            o_ref[...] = acc.astype(o_ref.dtype)


def _norm_proj(x, g, w, tabs, rope_full_tiles=(), rope_half_tiles=()):
    T, D = x.shape
    N = w.shape[1]
    tm, tn = min(PROJ_TM, T), PROJ_TN
    assert T % tm == 0 and N % tn == 0
    kern = functools.partial(_norm_proj_kernel, rope_full_tiles=tuple(rope_full_tiles),
                             rope_half_tiles=tuple(rope_half_tiles))
    tab_spec = pl.BlockSpec((tm, LANES), lambda i, j: (i, 0))
    return pl.pallas_call(
        kern,
        grid=(T // tm, N // tn),
        in_specs=[pl.BlockSpec((tm, D), lambda i, j: (i, 0)),
                  pl.BlockSpec((1, D), lambda i, j: (0, 0)),
                  pl.BlockSpec((D, tn), lambda i, j: (0, j)),
                  tab_spec, tab_spec, tab_spec],
        out_specs=pl.BlockSpec((tm, tn), lambda i, j: (i, j)),
        out_shape=jax.ShapeDtypeStruct((T, N), BF16),
        scratch_shapes=[pltpu.VMEM((tm, D), BF16)],
        compiler_params=_params(("parallel", "arbitrary")),
        name="norm_proj",
    )(x, g.reshape(1, D), w, *tabs)


def _flash_kernel(*refs, seq, qb, kb, window, use_sinks, use_sel):
    it = iter(refs)
    sinks_ref = next(it) if use_sinks else None
    q_ref, k_ref, v_ref = next(it), next(it), next(it)
    mb_ref = next(it) if use_sel else None
    e_ref = next(it) if use_sel else None
    o_ref, m_scr, acc_scr = next(it), next(it), next(it)

    pp = pl.program_id(1)
    i = pl.program_id(2)
    q0 = i * qb
    rows = GROUP * qb
    lane = lax.broadcasted_iota(jnp.int32, (qb, LANES), 1)
    r_minus_c = (lax.broadcasted_iota(jnp.int32, (qb, kb), 0)
                 - lax.broadcasted_iota(jnp.int32, (qb, kb), 1))
    ones_v = jnp.ones((kb, LANES), BF16)
    kt_hi = (q0 + qb - 1) // kb
    kt_lo = jnp.maximum(q0 - (window - 1), 0) // kb if window < seq else 0

    for par in (0, 1):
        own_half = (lane >= HEAD_DIM) if par else (lane < HEAD_DIM)
        parts = []
        for e in range(GROUP):
            tile = q_ref[:, e * LANES:(e + 1) * LANES]
            placed = jnp.where(own_half, tile, jnp.zeros_like(tile))
            if use_sel:
                placed = jnp.concatenate([placed, mb_ref[0, par]], axis=1)
            parts.append(placed)
        qs = jnp.concatenate(parts, axis=0)

        if use_sinks:
            for e in range(GROUP):
                sk = sinks_ref[(2 * pp + par) * GROUP + e]
                m_scr[e * qb:(e + 1) * qb, :] = jnp.full((qb, LANES), sk, F32)
            acc_scr[...] = jnp.concatenate(
                [jnp.zeros((rows, LANES), F32), jnp.ones((rows, LANES), F32)], axis=1)
        else:
            m_scr[...] = jnp.full((rows, LANES), NEG, F32)
            acc_scr[...] = jnp.zeros((rows, 2 * LANES), F32)

        def step(kt, masked):
            k0 = pl.multiple_of(kt * kb, kb)
            kt_tile = k_ref[pl.ds(k0, kb), :]
            if use_sel:
                kt_tile = jnp.concatenate([kt_tile, e_ref[pl.ds(k0, kb), :]], axis=1)
            s = lax.dot_general(qs, kt_tile, (((1,), (1,)), ((), ())),
                                preferred_element_type=F32)
            if masked:
                d = r_minus_c + (q0 - k0)
                vis = d >= 0
                if window < seq:
                    vis = vis & (d < window)
                s = jnp.where(vis[None], s.reshape(GROUP, qb, kb), NEG).reshape(rows, kb)
            m_prev = m_scr[...]
            m_new = jnp.maximum(m_prev, jnp.max(s, axis=1, keepdims=True))
            alpha = jnp.exp(m_prev - m_new)
            p = jnp.exp(s - jnp.concatenate([m_new] * (kb // LANES), axis=1)).astype(BF16)
            v_aug = jnp.concatenate([v_ref[pl.ds(k0, kb), :], ones_v], axis=1)
            pv = jnp.dot(p, v_aug, preferred_element_type=F32)
            acc_scr[...] = acc_scr[...] * jnp.concatenate([alpha, alpha], axis=1) + pv
            m_scr[...] = m_new

        if use_sel:
            def body(kt, c):
                step(kt, False)
                return c
            lax.fori_loop(0, kt_hi, body, 0)
            step(kt_hi, True)
        else:
            def body(kt, c):
                step(kt, True)
                return c
            lax.fori_loop(kt_lo, kt_hi + 1, body, 0)

        acc = acc_scr[...]
        o = acc[:, :LANES] / acc[:, LANES:]
        for e in range(GROUP):
            oe = o[e * qb:(e + 1) * qb].astype(o_ref.dtype)
            sl = slice(e * LANES, (e + 1) * LANES)
            if par == 0:
                o_ref[:, sl] = oe
            else:
                o_ref[:, sl] = jnp.where(own_half, oe, o_ref[:, sl])


def _flash(proj, batch, seq, k_col, v_col, window, sinks=None, mb=None, e_all=None):
    T = proj.shape[0]
    qb, kb = ATT_QB, ATT_KB
    nq = seq // qb
    use_sinks, use_sel = sinks is not None, mb is not None
    kern = functools.partial(_flash_kernel, seq=seq, qb=qb, kb=kb, window=window,
                             use_sinks=use_sinks, use_sel=use_sel)
    in_specs, args = [], []
    if use_sinks:
        in_specs.append(pl.BlockSpec(memory_space=pltpu.SMEM))
        args.append(sinks)
    in_specs += [pl.BlockSpec((qb, GROUP * LANES), lambda b, p, i: (b * nq + i, p)),
                 pl.BlockSpec((seq, LANES), lambda b, p, i: (b, k_col + p)),
                 pl.BlockSpec((seq, LANES), lambda b, p, i: (b, v_col + p))]
    args += [proj, proj, proj]
    if use_sel:
        in_specs += [pl.BlockSpec((1, 2, qb, LANES), lambda b, p, i: (b, p, i, 0)),
                     pl.BlockSpec((seq, LANES), lambda b, p, i: (0, 0))]
        args += [mb, e_all]
    return pl.pallas_call(
        kern,
        grid=(batch, 2, nq),
        in_specs=in_specs,
        out_specs=pl.BlockSpec((qb, GROUP * LANES), lambda b, p, i: (b * nq + i, p)),
        out_shape=jax.ShapeDtypeStruct((T, D_MODEL), BF16),
        scratch_shapes=[pltpu.VMEM((GROUP * qb, LANES), F32),
                        pltpu.VMEM((GROUP * qb, 2 * LANES), F32)],
        compiler_params=_params(("parallel", "parallel", "arbitrary")),
        name="flash_sel" if use_sel else ("flash_sink" if use_sinks else "flash_win"),
    )(*args)


def _gelu_tanh(x):
    return 0.5 * x * (1.0 + jnp.tanh(math.sqrt(2.0 / math.pi) * (x + 0.044715 * (x * x * x))))


def _compress_kernel(u_ref, pos_ref, w1_ref, w2_ref, o_ref):
    out = None
    half = w1_ref.shape[0] // 2
    for par in (0, 1):
        u = u_ref[0, par].astype(F32)
        lo = (u + pos_ref[0:1, :]).astype(BF16)
        hi = (u + pos_ref[1:2, :]).astype(BF16)
        a = jnp.dot(lo, w1_ref[:half, :], preferred_element_type=F32)
        b = jnp.dot(hi, w1_ref[half:, :], preferred_element_type=F32)
        b_next = jnp.concatenate([b[1:], jnp.zeros((1, b.shape[1]), F32)], axis=0)
        hid = _gelu_tanh(a + b_next).astype(BF16)
        term = jnp.dot(hid, w2_ref[par], preferred_element_type=F32)
        out = term if out is None else out + term
    o_ref[0, 0] = out.astype(o_ref.dtype)


def _compress(u, pos, w1, w2_pair):
    batch, _, chunks, feat = u.shape
    return pl.pallas_call(
        _compress_kernel,
        grid=(batch, 2),
        in_specs=[pl.BlockSpec((1, 2, chunks, feat), lambda b, p: (b, p, 0, 0)),
                  pl.BlockSpec((2, feat), lambda b, p: (0, 0)),
                  pl.BlockSpec((2 * feat, CMP_HIDDEN), lambda b, p: (0, 0)),
                  pl.BlockSpec((2, CMP_HIDDEN, LANES), lambda b, p: (0, 0, 0))],
        out_specs=pl.BlockSpec((1, 1, chunks, LANES), lambda b, p: (b, p, 0, 0)),
        out_shape=jax.ShapeDtypeStruct((batch, 2, chunks, LANES), BF16),
        compiler_params=_params(("parallel", "parallel")),
        name="compress",
    )(u, pos, w1, w2_pair)


def _cmp_select_kernel(q_ref, kc_ref, vc_ref, ovt_ref, o_ref, mb_ref, *, qb, n_cmp, n_sel, n_top):
    i = pl.program_id(2)
    q0 = i * qb
    nck = kc_ref.shape[2]
    rows = GROUP * qb
    lane = lax.broadcasted_iota(jnp.int32, (qb, LANES), 1)
    t_row = q0 + lax.broadcasted_iota(jnp.int32, (qb, nck), 0)
    c_col = lax.broadcasted_iota(jnp.int32, (qb, nck), 1)
    valid = (c_col * CMP_STRIDE + (CMP_BLOCK - 1) <= t_row) & (c_col < n_cmp)
    blk = lax.broadcasted_iota(jnp.int32, (n_sel, qb), 0)
    cur = (q0 + lax.broadcasted_iota(jnp.int32, (n_sel, qb), 1)) // SLC_BLOCK
    eye = (lax.broadcasted_iota(jnp.int32, (qb, qb), 0)
           == lax.broadcasted_iota(jnp.int32, (qb, qb), 1)).astype(BF16)
    kc = kc_ref[0, 0]
    vc = vc_ref[0, 0]

    for par in (0, 1):
        own_half = (lane >= HEAD_DIM) if par else (lane < HEAD_DIM)
        parts = []
        for e in range(GROUP):
            tile = q_ref[:, e * LANES:(e + 1) * LANES]
            parts.append(jnp.where(own_half, tile, jnp.zeros_like(tile)))
        qs = jnp.concatenate(parts, axis=0)
        s = lax.dot_general(qs, kc, (((1,), (1,)), ((), ())), preferred_element_type=F32)
        s = jnp.where(valid[None], s.reshape(GROUP, qb, nck), NEG)
        m = jnp.max(s, axis=-1, keepdims=True)
        ex = jnp.where(valid[None], jnp.exp(s - m), 0.0)
        p = ex / jnp.maximum(jnp.sum(ex, axis=-1, keepdims=True), 1e-30)
        o = jnp.dot(p.reshape(rows, nck).astype(BF16), vc, preferred_element_type=F32)
        for e in range(GROUP):
            oe = o[e * qb:(e + 1) * qb].astype(o_ref.dtype)
            sl = slice(e * LANES, (e + 1) * LANES)
            if par == 0:
                o_ref[:, sl] = oe
            else:
                o_ref[:, sl] = jnp.where(own_half, oe, o_ref[:, sl])

        p_sum = jnp.sum(p, axis=0)
        imp_t = jnp.zeros((n_sel, qb), F32)
        rest = p_sum
        for _ in range(3):
            piece = rest.astype(BF16)
            rest = rest - piece.astype(F32)
            imp_t = imp_t + lax.dot_general(ovt_ref[...], piece, (((1,), (1,)), ((), ())),
                                            preferred_element_type=F32)
        forced = (blk == 0) | (blk == cur) | (blk == cur - 1)
        imp_t = jnp.where(forced, jnp.inf, imp_t)
        imp_t = jnp.where(blk > cur, -jnp.inf, imp_t)
        rank = jnp.zeros((n_sel, qb), jnp.int32)
        for r in range(n_sel):
            row = imp_t[r:r + 1, :]
            before = (row > imp_t) | ((row == imp_t) & (blk > r))
            rank = rank + before.astype(jnp.int32)
        unsel_t = jnp.where(rank < n_top, 0.0, 1.0).astype(BF16)
        unsel_t = jnp.concatenate([unsel_t, jnp.zeros((LANES - n_sel, qb), BF16)], axis=0)
        unsel = lax.dot_general(eye, unsel_t, (((1,), (1,)), ((), ())),
                                preferred_element_type=F32)
        mb_ref[0, par] = (unsel * NEG).astype(mb_ref.dtype)


def _cmp_select(proj, kcmp, vcmp, ovt, batch, seq, n_cmp):
    T = proj.shape[0]
    qb = ATT_QB
    nq = seq // qb
    nck = kcmp.shape[2]
    n_sel = seq // SLC_BLOCK
    assert n_sel <= LANES
    kern = functools.partial(_cmp_select_kernel, qb=qb, n_cmp=n_cmp, n_sel=n_sel,
                             n_top=min(SLC_TOPK, n_sel))
    return pl.pallas_call(
        kern,
        grid=(batch, 2, nq),
        in_specs=[pl.BlockSpec((qb, GROUP * LANES), lambda b, p, i: (b * nq + i, p)),
                  pl.BlockSpec((1, 1, nck, LANES), lambda b, p, i: (b, p, 0, 0)),
                  pl.BlockSpec((1, 1, nck, LANES), lambda b, p, i: (b, p, 0, 0)),
                  pl.BlockSpec((n_sel, nck), lambda b, p, i: (0, 0))],
        out_specs=[pl.BlockSpec((qb, GROUP * LANES), lambda b, p, i: (b * nq + i, p)),
                   pl.BlockSpec((1, 2, qb, LANES), lambda b, p, i: (b, p, i, 0))],
        out_shape=[jax.ShapeDtypeStruct((T, D_MODEL), BF16),
                   jax.ShapeDtypeStruct((batch, N_GROUPS, seq, LANES), BF16)],
        compiler_params=_params(("parallel", "parallel", "arbitrary")),
        name="cmp_select",
    )(proj, kcmp, vcmp, ovt)


def _silu(z):
    return z / (1.0 + jnp.exp(-z))


def _out_proj_kernel(*refs, mode, tm, seq, final_norm):
    it = iter(refs)
    x_ref = next(it)
    if mode == "a":
        o_ref, z_ref = next(it), next(it)
        a = o_ref[...].astype(F32) * _silu(z_ref[...].astype(F32))
    elif mode == "b":
        oc_ref, os_ref, ow_ref, z_ref, g_ref, ex_ref = (next(it) for _ in range(6))
        g = 1.0 / (1.0 + jnp.exp(-g_ref[...].astype(F32)))
        g_hi = g.astype(BF16)
        g_lo = (g - g_hi.astype(F32)).astype(BF16)
        o = None
        for br, ref in enumerate((oc_ref, os_ref, ow_ref)):
            ge = (jnp.dot(g_hi, ex_ref[br], preferred_element_type=F32)
                  + jnp.dot(g_lo, ex_ref[br], preferred_element_type=F32))
            term = ge * ref[...].astype(F32)
            o = term if o is None else o + term
        a = o * _silu(z_ref[...].astype(F32))
    else:
        u_ref, bg_ref, c_ref, z_ref, uh_ref, ch_ref, cw_ref = (next(it) for _ in range(7))
        v = c_ref[...].astype(F32) * u_ref[...].astype(F32)
        first = (pl.program_id(0) * tm) % seq == 0
        vh = ch_ref[...].astype(F32) * uh_ref[...].astype(F32)
        vh = jnp.where(first, 0.0, vh)
        h1, h2 = vh[7:8, :], vh[6:7, :]
        row = lax.broadcasted_iota(jnp.int32, v.shape, 0)
        v1 = jnp.where(row == 0, h1, pltpu.roll(v, 1, 0))
        v2 = jnp.where(row == 0, h2, jnp.where(row == 1, h1, pltpu.roll(v, 2, 0)))
        y = cw_ref[0:1, :] * v2 + cw_ref[1:2, :] * v1 + cw_ref[2:3, :] * v
        a = bg_ref[...].astype(F32) * y * _silu(z_ref[...].astype(F32))
    w_ref = next(it)
    fw_ref = next(it) if final_norm else None
    out_ref = next(it)
    xn = x_ref[...] + jnp.dot(a.astype(BF16), w_ref[...], preferred_element_type=F32)
    if final_norm:
        ms = jnp.mean(xn * xn, axis=-1, keepdims=True)
        xn = xn * lax.rsqrt(ms + NORM_EPS) * fw_ref[...]
    out_ref[...] = xn


def _out_proj(x, w_out, mode, inputs, seq, final_w=None):
    T, D = x.shape
    tm = OUT_TM
    assert T % tm == 0 and seq % tm == 0
    row = lambda c: pl.BlockSpec((tm, D), lambda i, c=c: (i, c))
    in_specs, args = [row(0)], [x]
    if mode == "a":
        o, proj = inputs
        in_specs += [row(0), row(1)]
        args += [o, proj]
    elif mode == "b":
        oc, osl, ow, proj, gate_col, expand = inputs
        in_specs += [row(0), row(0), row(0), row(1),
                     pl.BlockSpec((tm, LANES), lambda i: (i, gate_col)),
                     pl.BlockSpec(expand.shape, lambda i: (0, 0, 0))]
        args += [oc, osl, ow, proj, proj, expand]
    else:
        proj, conv_w = inputs
        halo = lambda c: pl.BlockSpec((8, D), lambda i, c=c: (jnp.maximum(i * (tm // 8) - 1, 0), c))
        in_specs += [row(0), row(1), row(2), row(3), halo(0), halo(2),
                     pl.BlockSpec(conv_w.shape, lambda i: (0, 0))]
        args += [proj, proj, proj, proj, proj, proj, conv_w]
    in_specs.append(pl.BlockSpec((D, D), lambda i: (0, 0)))
    args.append(w_out)
    if final_w is not None:
        in_specs.append(pl.BlockSpec((1, D), lambda i: (0, 0)))
        args.append(final_w.reshape(1, D))
    kern = functools.partial(_out_proj_kernel, mode=mode, tm=tm, seq=seq,
                             final_norm=final_w is not None)
    return pl.pallas_call(
        kern,
        grid=(T // tm,),
        in_specs=in_specs,
        out_specs=pl.BlockSpec((tm, D), lambda i: (i, 0)),
        out_shape=jax.ShapeDtypeStruct((T, D), F32),
        compiler_params=_params(("parallel",)),
        name="out_proj_" + mode,
    )(*args)


def _head_perm():
    heads = []
    for pair in range(2):
        for e in range(GROUP):
            for par in range(2):
                heads.append((2 * pair + par) * GROUP + e)
    return np.asarray(heads, np.int32)


_HEADS = _head_perm()
_COLS = (_HEADS[:, None] * HEAD_DIM + np.arange(HEAD_DIM, dtype=np.int32)[None, :]).reshape(-1)


def _rope_tables(positions):
    inv_freq = ROPE_THETA ** (-jnp.arange(0, ROPE_DIM, 2, dtype=F32) / ROPE_DIM)
    ang = positions.astype(F32).reshape(-1, 1) * inv_freq[None, :]
    cos, sin = jnp.cos(ang), jnp.sin(ang)
    T = ang.shape[0]
    rest = HEAD_DIM - ROPE_DIM
    cos_t = jnp.concatenate([cos, cos, jnp.ones((T, rest), F32)], axis=1)
    sin_a = jnp.concatenate([-sin, jnp.zeros((T, HEAD_DIM - ROPE_HALF), F32)], axis=1)
    sin_b = jnp.concatenate([jnp.zeros((T, ROPE_HALF), F32), sin, jnp.zeros((T, rest), F32)], axis=1)
    return tuple(jnp.concatenate([t, t], axis=1) for t in (cos_t, sin_a, sin_b))


def _overlap_t(seq):
    nc = (seq - CMP_BLOCK) // CMP_STRIDE + 1
    nsel = seq // SLC_BLOCK
    c_start = np.arange(nc) * CMP_STRIDE
    c_end = c_start + CMP_BLOCK
    s_start = np.arange(nsel) * SLC_BLOCK
    s_end = s_start + SLC_BLOCK
    ov = np.clip(np.minimum(c_end[:, None], s_end[None, :]) - np.maximum(c_start[:, None], s_start[None, :]), 0, None)
    ov = (ov / CMP_BLOCK).astype(np.float32)
    ovt = np.zeros((nsel, seq // CMP_STRIDE), np.float32)
    ovt[:, :nc] = ov.T
    return jnp.asarray(ovt, BF16), nc


def _mixer_a(x, norm_g, w_in, sinks, w_out, tabs, batch, seq, final_w=None):
    q, k, v, z = (w_in[:, :2048], w_in[:, 2048:2304], w_in[:, 2304:2560], w_in[:, 2560:])
    w = jnp.concatenate([q[:, _COLS] * (HEAD_DIM ** -0.5), z[:, _COLS], k, v], axis=1).astype(BF16)
    proj = _norm_proj(x, norm_g, w, tabs, rope_full_tiles=(0, 1, 2, 3), rope_half_tiles=(8,))
    o = _flash(proj, batch, seq, k_col=32, v_col=34, window=A_WINDOW, sinks=sinks)
    return _out_proj(x, w_out[_COLS, :].astype(BF16), "a", (o, proj), seq, final_w)


def _mixer_b(x, norm_g, w_in, kc_pos, kc_w1, kc_w2, vc_pos, vc_w1, vc_w2, w_out, tabs, batch, seq,
             final_w=None):
    T = x.shape[0]
    c = [2048 + 256 * n for n in range(7)]
    q = w_in[:, :2048]
    kc, vc, ks, vs, kw, vw = (w_in[:, c[n]:c[n + 1]] for n in range(6))
    gates = w_in[:, c[6]:c[6] + 96]
    z = w_in[:, c[6] + 96:]
    gate_cols = np.concatenate([br * N_HEADS + _HEADS for br in range(3)])
    w = jnp.concatenate([q[:, _COLS] * (HEAD_DIM ** -0.5), z[:, _COLS], ks, kw, kc, vc, vs, vw,
                         gates[:, gate_cols], jnp.zeros((D_MODEL, PROJ_TN - 96), F32)], axis=1).astype(BF16)
    proj = _norm_proj(x, norm_g, w, tabs, rope_full_tiles=(0, 1, 2, 3, 8))

    chunks = seq // CMP_STRIDE

    def chunked(col):
        t = proj[:, col:col + 256].reshape(batch, chunks, CMP_STRIDE, N_GROUPS, HEAD_DIM)
        return t.transpose(0, 3, 1, 2, 4).reshape(batch, N_GROUPS, chunks, CMP_STRIDE * HEAD_DIM)

    def w2_pair(w2):
        zero = jnp.zeros_like(w2)
        return jnp.stack([jnp.concatenate([w2, zero], axis=1),
                          jnp.concatenate([zero, w2], axis=1)]).astype(BF16)

    half = CMP_STRIDE * HEAD_DIM
    kcmp = _compress(chunked(4608), kc_pos.reshape(2, half), kc_w1.astype(BF16), w2_pair(kc_w2))
    vcmp = _compress(chunked(4864), vc_pos.reshape(2, half), vc_w1.astype(BF16), w2_pair(vc_w2))

    ovt, n_cmp = _overlap_t(seq)
    o_cmp, mb = _cmp_select(proj, kcmp, vcmp, ovt, batch, seq, n_cmp)
    e_all = jnp.asarray((np.arange(seq)[:, None] // SLC_BLOCK) == np.arange(LANES)[None, :], BF16)
    o_slc = _flash(proj, batch, seq, k_col=32, v_col=40, window=seq, mb=mb, e_all=e_all)
    o_win = _flash(proj, batch, seq, k_col=34, v_col=42, window=B_WINDOW)

    expand = np.zeros((3, LANES, D_MODEL), np.float32)
    for br in range(3):
        for slot in range(N_HEADS):
            expand[br, br * N_HEADS + slot, slot * HEAD_DIM:(slot + 1) * HEAD_DIM] = 1.0
    return _out_proj(x, w_out[_COLS, :].astype(BF16), "b",
                     (o_cmp, o_slc, o_win, proj, 44, jnp.asarray(expand, BF16)), seq, final_w)


def _mixer_c(x, norm_g, w_in, conv_w, w_out, tabs, seq, final_w=None):
    proj = _norm_proj(x, norm_g, w_in.astype(BF16), tabs)
    cw = jnp.concatenate([conv_w, jnp.zeros((8 - conv_w.shape[0], conv_w.shape[1]), F32)], axis=0)
    return _out_proj(x, w_out.astype(BF16), "c", (proj, cw), seq, final_w)


def kernel(x, positions, norm_w, final_norm_w, a_w_in, a_sinks, a_w_out, b_w_in, b_cmp_k_pos, b_cmp_k_w1,
           b_cmp_k_w2, b_cmp_v_pos, b_cmp_v_w1, b_cmp_v_w2, b_w_out, c_w_in, c_conv_w, c_w_out):
    batch, seq, d = x.shape
    depth = norm_w.shape[0]
    tabs = _rope_tables(positions)
    xf = x.reshape(batch * seq, d)
    for i in range(depth):
        kind, j = i % 3, i // 3
        fw = final_norm_w if i == depth - 1 else None
        if kind == 0:
            xf = _mixer_a(xf, norm_w[i], a_w_in[j], a_sinks[j], a_w_out[j], tabs, batch, seq, fw)
        elif kind == 1:
            xf = _mixer_b(xf, norm_w[i], b_w_in[j], b_cmp_k_pos[j], b_cmp_k_w1[j], b_cmp_k_w2[j],
                          b_cmp_v_pos[j], b_cmp_v_w1[j], b_cmp_v_w2[j], b_w_out[j], tabs, batch, seq, fw)
        else:
            xf = _mixer_c(xf, norm_w[i], c_w_in[j], c_conv_w[j], c_w_out[j], tabs, seq, fw)
    return xf.reshape(batch, seq, d)
```

```python
import functools
import math

import numpy as np
import jax
import jax.numpy as jnp
from jax import lax
from jax.experimental import pallas as pl
from jax.experimental.pallas import tpu as pltpu

F32 = jnp.float32
BF16 = jnp.bfloat16

D_MODEL = 2048
HEAD_DIM = 64
N_HEADS = 32
N_GROUPS = 4
GROUP = 8
ROPE_DIM = 16
ROPE_HALF = 8
ROPE_THETA = 500000.0
NORM_EPS = 1e-5
A_WINDOW = 128
B_WINDOW = 512
CMP_BLOCK = 32
CMP_STRIDE = 16
SLC_BLOCK = 64
SLC_TOPK = 16
CMP_HIDDEN = 256

LANES = 128
NEG = -2.0 ** 100
LOG2E = math.log2(math.e)
VMEM_LIMIT = 56 * 1024 * 1024

PROJ_TM = 1024
PROJ_TN = 512
OUT_TM = 256
ATT_QB = 128
ACC_ROWS = HEAD_DIM + 16


def _params(sem):
    return pltpu.CompilerParams(dimension_semantics=sem, vmem_limit_bytes=VMEM_LIMIT)


def _norm_proj_kernel(x_ref, g_ref, w_ref, cos_ref, sa_ref, sb_ref, o_ref, h_ref, *,
                      rope_full_tiles, rope_half_tiles):
    j = pl.program_id(1)

    @pl.when(j == 0)
    def _():
        x = x_ref[...]
        ms = jnp.mean(x * x, axis=-1, keepdims=True)
        h_ref[...] = (x * lax.rsqrt(ms + NORM_EPS) * g_ref[...]).astype(BF16)

    acc = jnp.dot(h_ref[...], w_ref[...], preferred_element_type=F32)
    tn = acc.shape[1]

    def rope(a):
        n = a.shape[1]
        reps = n // LANES
        cos = jnp.concatenate([cos_ref[...]] * reps, axis=1)
        sa = jnp.concatenate([sa_ref[...]] * reps, axis=1)
        sb = jnp.concatenate([sb_ref[...]] * reps, axis=1)
        return a * cos + pltpu.roll(a, n - ROPE_HALF, 1) * sa + pltpu.roll(a, ROPE_HALF, 1) * sb

    def any_of(tiles):
        c = j == tiles[0]
        for t in tiles[1:]:
            c = c | (j == t)
        return c

    plain = None
    if rope_full_tiles:
        full = any_of(rope_full_tiles)
        plain = ~full

        @pl.when(full)
        def _():
            o_ref[...] = rope(acc).astype(o_ref.dtype)

    if rope_half_tiles:
        half = any_of(rope_half_tiles)
        plain = ~half if plain is None else plain & ~half

        @pl.when(half)
        def _():
            hw = tn // 2
            o_ref[:, :hw] = rope(acc[:, :hw]).astype(o_ref.dtype)
            o_ref[:, hw:] = acc[:, hw:].astype(o_ref.dtype)

    if plain is None:
        o_ref[...] = acc.astype(o_ref.dtype)
    else:
        @pl.when(plain)
        def _():
            o_ref[...] = acc.astype(o_ref.dtype)


def _norm_proj(x, g, w, tabs, rope_full_tiles=(), rope_half_tiles=()):
    T, D = x.shape
    N = w.shape[1]
    tm, tn = min(PROJ_TM, T), PROJ_TN
    assert T % tm == 0 and N % tn == 0
    kern = functools.partial(_norm_proj_kernel, rope_full_tiles=tuple(rope_full_tiles),
                             rope_half_tiles=tuple(rope_half_tiles))
    tab_spec = pl.BlockSpec((tm, LANES), lambda i, j: (i, 0))
    return pl.pallas_call(
        kern,
        grid=(T // tm, N // tn),
        in_specs=[pl.BlockSpec((tm, D), lambda i, j: (i, 0)),
                  pl.BlockSpec((1, D), lambda i, j: (0, 0)),
                  pl.BlockSpec((D, tn), lambda i, j: (0, j)),
                  tab_spec, tab_spec, tab_spec],
        out_specs=pl.BlockSpec((tm, tn), lambda i, j: (i, j)),
        out_shape=jax.ShapeDtypeStruct((T, N), BF16),
        scratch_shapes=[pltpu.VMEM((tm, D), BF16)],
        compiler_params=_params(("parallel", "arbitrary")),
        name="norm_proj",
    )(x, g.reshape(1, D), w, *tabs)


def _placed_queries(qt_ref, hp, par, qb):
    feat = lax.broadcasted_iota(jnp.int32, (LANES, qb), 0)
    own = (feat >= HEAD_DIM) if par else (feat < HEAD_DIM)
    cols = []
    for e in (2 * hp, 2 * hp + 1):
        t = qt_ref[0, 0, e]
        cols.append(jnp.where(own, t, jnp.zeros_like(t)))
    return cols


def _store_token_major(o_ref, get_ot, qb):
    eye = (lax.broadcasted_iota(jnp.int32, (qb, qb), 0)
           == lax.broadcasted_iota(jnp.int32, (qb, qb), 1)).astype(BF16)
    for e in range(GROUP):
        hp, sub = divmod(e, 2)
        ot = jnp.concatenate([get_ot(par, hp)[:, sub * qb:(sub + 1) * qb] for par in (0, 1)],
                             axis=0).astype(BF16)
        o = lax.dot_general(eye, ot, (((1,), (1,)), ((), ())), preferred_element_type=F32)
        o_ref[:, e * LANES:(e + 1) * LANES] = o.astype(o_ref.dtype)


def _flash_kernel(*refs, seq, qb, kb, window, use_sinks, use_sel):
    it = iter(refs)
    sinks_ref = next(it) if use_sinks else None
    qt_ref, k_ref, vt_ref = next(it), next(it), next(it)
    mbt_ref = next(it) if use_sel else None
    e_ref = next(it) if use_sel else None
    o_ref, qa_scr, m_scr, acc_scr, st_scr, p_scr, al_scr = (next(it) for _ in range(7))

    pp = pl.program_id(1)
    i = pl.program_id(2)
    q0 = i * qb
    ch = 2 * qb
    n_chain = 2 * (GROUP // 2)

    half = (GROUP // 2) * ch
    lane = lax.broadcasted_iota(jnp.int32, (1, ch), 1)
    for c in range(n_chain):
        par, hp = divmod(c, GROUP // 2)
        cols = _placed_queries(qt_ref, hp, par, qb)
        if use_sel:
            cols = [jnp.concatenate([t, mbt_ref[0, par]], axis=0) for t in cols]
        qa_scr[:, c * ch:(c + 1) * ch] = jnp.concatenate(cols, axis=1)
        if use_sinks:
            head = (2 * pp + par) * GROUP + 2 * hp
            m_scr[:, c * ch:(c + 1) * ch] = (
                jnp.where(lane < qb, sinks_ref[head], sinks_ref[head + 1]) * LOG2E)
    acc_row = lax.broadcasted_iota(jnp.int32, (ACC_ROWS, half), 0)
    for par in (0, 1):
        acc_scr[par] = (jnp.where(acc_row == HEAD_DIM, 1.0, 0.0) if use_sinks
                        else jnp.zeros((ACC_ROWS, half), F32))
    if not use_sinks:
        m_scr[...] = jnp.full((1, 2 * half), NEG, F32)

    q_minus_k = (lax.rem(lax.broadcasted_iota(jnp.int32, (kb, ch), 1), qb)
                 - lax.broadcasted_iota(jnp.int32, (kb, ch), 0))
    ones_rows = jnp.where(lax.broadcasted_iota(jnp.int32, (ACC_ROWS - HEAD_DIM, kb), 0) == 0,
                          1.0, 0.0).astype(BF16)
    kt_hi = (q0 + qb - 1) // kb
    kt_lo = jnp.maximum(q0 - (window - 1), 0) // kb if window < seq else 0

    def step(kt, masked):
        k0 = pl.multiple_of(kt * kb, kb)
        ka = k_ref[pl.ds(k0, kb), :]
        if use_sel:
            ka = jnp.concatenate([ka, e_ref[pl.ds(k0, kb), :]], axis=1)
        vt = vt_ref[0, 0, kt]
        if masked:
            d = q_minus_k + (q0 - k0)
            vis = d >= 0
            if window < seq:
                vis = vis & (d < window)
            bias = jnp.where(vis, 0.0, NEG)
        for c in range(n_chain):
            st = jnp.dot(ka, qa_scr[:, c * ch:(c + 1) * ch], preferred_element_type=F32)
            st_scr[:, c * ch:(c + 1) * ch] = st + bias if masked else st
        for c in range(n_chain):
            par, hp = divmod(c, GROUP // 2)
            for col in range(c * ch, (c + 1) * ch, LANES):
                sl = slice(col, col + LANES)
                x = st_scr[:, sl]
                m_prev = m_scr[:, sl]
                m_new = jnp.maximum(m_prev, jnp.max(x, axis=0, keepdims=True))
                al_scr[:, sl] = jnp.exp2(m_prev - m_new)
                m_scr[:, sl] = m_new
                p_scr[:, sl] = jnp.exp2(x - m_new).astype(BF16)
            sl = slice(c * ch, (c + 1) * ch)
            vta = jnp.concatenate([vt[par * HEAD_DIM:(par + 1) * HEAD_DIM], ones_rows], axis=0)
            pv = jnp.dot(vta, p_scr[:, sl], preferred_element_type=F32)
            asl = slice(hp * ch, (hp + 1) * ch)
            acc_scr[par, :, asl] = acc_scr[par, :, asl] * al_scr[:, sl] + pv

    if use_sel:
        def body(kt, carry):
            step(kt, False)
            return carry
        lax.fori_loop(0, kt_hi, body, 0)
        step(kt_hi, True)
    else:
        def body(kt, carry):
            step(kt, True)
            return carry
        lax.fori_loop(kt_lo, kt_hi + 1, body, 0)

    def get_ot(par, hp):
        acc = acc_scr[par][:, hp * ch:(hp + 1) * ch]
        return acc[:HEAD_DIM] * (1.0 / acc[HEAD_DIM:HEAD_DIM + 1])

    _store_token_major(o_ref, get_ot, qb)


def _flash(qt, proj, vt, batch, seq, k_col, window, kb, sinks=None, mbt=None, e_all=None):
    T = proj.shape[0]
    qb = ATT_QB
    nq = seq // qb
    ch = 2 * qb
    use_sinks, use_sel = sinks is not None, mbt is not None
    kern = functools.partial(_flash_kernel, seq=seq, qb=qb, kb=kb, window=window,
                             use_sinks=use_sinks, use_sel=use_sel)
    in_specs, args = [], []
    if use_sinks:
        in_specs.append(pl.BlockSpec(memory_space=pltpu.SMEM))
        args.append(sinks)
    in_specs += [pl.BlockSpec((1, 1, GROUP, LANES, qb), lambda b, p, i: (b, p, 0, 0, i)),
                 pl.BlockSpec((seq, LANES), lambda b, p, i: (b, k_col + p)),
                 pl.BlockSpec((1, 1, seq // kb, LANES, kb), lambda b, p, i: (b, p, 0, 0, 0))]
    args += [qt, proj, vt]
    if use_sel:
        in_specs += [pl.BlockSpec((1, 2, LANES, qb), lambda b, p, i: (b, p, 0, i)),
                     pl.BlockSpec((seq, LANES), lambda b, p, i: (0, 0))]
        args += [mbt, e_all]
    n_chain = GROUP
    return pl.pallas_call(
        kern,
        grid=(batch, 2, nq),
        in_specs=in_specs,
        out_specs=pl.BlockSpec((qb, GROUP * LANES), lambda b, p, i: (b * nq + i, p)),
        out_shape=jax.ShapeDtypeStruct((T, D_MODEL), BF16),
        scratch_shapes=[pltpu.VMEM((2 * LANES if use_sel else LANES, n_chain * ch), BF16),
                        pltpu.VMEM((1, n_chain * ch), F32),
                        pltpu.VMEM((2, ACC_ROWS, n_chain * ch // 2), F32),
                        pltpu.VMEM((kb, n_chain * ch), F32),
                        pltpu.VMEM((kb, n_chain * ch), BF16),
                        pltpu.VMEM((1, n_chain * ch), F32)],
        compiler_params=_params(("parallel", "parallel", "arbitrary")),
        name="flash_sel" if use_sel else ("flash_sink" if use_sinks else "flash_win"),
    )(*args)


def _gelu_tanh(x):
    return 0.5 * x * (1.0 + jnp.tanh(math.sqrt(2.0 / math.pi) * (x + 0.044715 * (x * x * x))))


def _compress_kernel(u_ref, pos_ref, w1_ref, w2_ref, o_ref, *, transposed):
    out = None
    half = w1_ref.shape[0] // 2
    for par in (0, 1):
        u = u_ref[0, par].astype(F32)
        lo = (u + pos_ref[0:1, :]).astype(BF16)
        hi = (u + pos_ref[1:2, :]).astype(BF16)
        a = jnp.dot(lo, w1_ref[:half, :], preferred_element_type=F32)
        b = jnp.dot(hi, w1_ref[half:, :], preferred_element_type=F32)
        b_next = jnp.concatenate([b[1:], jnp.zeros((1, b.shape[1]), F32)], axis=0)
        hid = _gelu_tanh(a + b_next).astype(BF16)
        if transposed:
            term = lax.dot_general(w2_ref[par], hid, (((1,), (1,)), ((), ())),
                                   preferred_element_type=F32)
        else:
            term = jnp.dot(hid, w2_ref[par], preferred_element_type=F32)
        out = term if out is None else out + term
    o_ref[0, 0] = out.astype(o_ref.dtype)


def _compress(u, pos, w1, w2_pair, transposed):
    batch, _, chunks, feat = u.shape
    oshape = (LANES, chunks) if transposed else (chunks, LANES)
    return pl.pallas_call(
        functools.partial(_compress_kernel, transposed=transposed),
        grid=(batch, 2),
        in_specs=[pl.BlockSpec((1, 2, chunks, feat), lambda b, p: (b, p, 0, 0)),
                  pl.BlockSpec((2, feat), lambda b, p: (0, 0)),
                  pl.BlockSpec((2 * feat, CMP_HIDDEN), lambda b, p: (0, 0)),
                  pl.BlockSpec(w2_pair.shape, lambda b, p: (0, 0, 0))],
        out_specs=pl.BlockSpec((1, 1) + oshape, lambda b, p: (b, p, 0, 0)),
        out_shape=jax.ShapeDtypeStruct((batch, 2) + oshape, BF16),
        compiler_params=_params(("parallel", "parallel")),
        name="compress",
    )(u, pos, w1, w2_pair)


def _cmp_select_kernel(qt_ref, kc_ref, vct_ref, ovt_ref, o_ref, mbt_ref, ot_scr, *,
                       qb, n_cmp, n_sel, n_top):
    i = pl.program_id(2)
    q0 = i * qb
    nck = kc_ref.shape[2]
    ch = 2 * qb
    c_sub = lax.broadcasted_iota(jnp.int32, (nck, ch), 0)
    t_lane = q0 + lax.rem(lax.broadcasted_iota(jnp.int32, (nck, ch), 1), qb)
    valid = (c_sub * CMP_STRIDE + (CMP_BLOCK - 1) <= t_lane) & (c_sub < n_cmp)
    blk = lax.broadcasted_iota(jnp.int32, (n_sel, qb), 0)
    cur = (q0 + lax.broadcasted_iota(jnp.int32, (n_sel, qb), 1)) // SLC_BLOCK
    kc = kc_ref[0, 0]

    for par in (0, 1):
        vt = vct_ref[0, 0][par * HEAD_DIM:(par + 1) * HEAD_DIM]
        p_sum = jnp.zeros((nck, qb), F32)
        for hp in range(GROUP // 2):
            qa = jnp.concatenate(_placed_queries(qt_ref, hp, par, qb), axis=1)
            st = jnp.dot(kc, qa, preferred_element_type=F32)
            st = jnp.where(valid, st, NEG)
            m = jnp.max(st, axis=0, keepdims=True)
            ex = jnp.where(valid, jnp.exp2(st - m), 0.0)
            p = ex * (1.0 / jnp.maximum(jnp.sum(ex, axis=0, keepdims=True), 1e-30))
            ot_scr[par, hp] = jnp.dot(vt, p.astype(BF16), preferred_element_type=F32)
            p_sum = p_sum + p[:, :qb] + p[:, qb:]

        imp_t = jnp.zeros((n_sel, qb), F32)
        rest = p_sum
        for _ in range(3):
            piece = rest.astype(BF16)
            rest = rest - piece.astype(F32)
            imp_t = imp_t + jnp.dot(ovt_ref[...], piece, preferred_element_type=F32)
        forced = (blk == 0) | (blk == cur) | (blk == cur - 1)
        imp_t = jnp.where(forced, jnp.inf, imp_t)
        imp_t = jnp.where(blk > cur, -jnp.inf, imp_t)
        rank = jnp.zeros((n_sel, qb), jnp.int32)
        for r in range(n_sel):
            row = imp_t[r:r + 1, :]
            before = (row > imp_t) | ((row == imp_t) & (blk > r))
            rank = rank + before.astype(jnp.int32)
        bias_t = jnp.where(rank < n_top, 0.0, NEG)
        mbt_ref[0, par] = jnp.concatenate(
            [bias_t, jnp.zeros((LANES - n_sel, qb), F32)], axis=0).astype(mbt_ref.dtype)

    _store_token_major(o_ref, lambda par, hp: ot_scr[par, hp], qb)


def _cmp_select(qt, kcmp, vcmp_t, ovt, batch, seq, n_cmp):
    qb = ATT_QB
    nq = seq // qb
    nck = kcmp.shape[2]
    n_sel = seq // SLC_BLOCK
    assert n_sel <= LANES
    kern = functools.partial(_cmp_select_kernel, qb=qb, n_cmp=n_cmp, n_sel=n_sel,
                             n_top=min(SLC_TOPK, n_sel))
    return pl.pallas_call(
        kern,
        grid=(batch, 2, nq),
        in_specs=[pl.BlockSpec((1, 1, GROUP, LANES, qb), lambda b, p, i: (b, p, 0, 0, i)),
                  pl.BlockSpec((1, 1, nck, LANES), lambda b, p, i: (b, p, 0, 0)),
                  pl.BlockSpec((1, 1, LANES, nck), lambda b, p, i: (b, p, 0, 0)),
                  pl.BlockSpec((n_sel, nck), lambda b, p, i: (0, 0))],
        out_specs=[pl.BlockSpec((qb, GROUP * LANES), lambda b, p, i: (b * nq + i, p)),
                   pl.BlockSpec((1, 2, LANES, qb), lambda b, p, i: (b, p, 0, i))],
        out_shape=[jax.ShapeDtypeStruct((batch * seq, D_MODEL), BF16),
                   jax.ShapeDtypeStruct((batch, N_GROUPS, LANES, seq), BF16)],
        scratch_shapes=[pltpu.VMEM((2, GROUP // 2, HEAD_DIM, 2 * qb), F32)],
        compiler_params=_params(("parallel", "parallel", "arbitrary")),
        name="cmp_select",
    )(qt, kcmp, vcmp_t, ovt)


def _silu(z):
    return z / (1.0 + jnp.exp(-z))


def _out_proj_kernel(*refs, mode, tm, seq, final_norm):
    it = iter(refs)
    x_ref = next(it)
    if mode == "a":
        o_ref, z_ref = next(it), next(it)
        a = o_ref[...].astype(F32) * _silu(z_ref[...].astype(F32))
    elif mode == "b":
        oc_ref, os_ref, ow_ref, z_ref, g_ref, ex_ref = (next(it) for _ in range(6))
        g = 1.0 / (1.0 + jnp.exp(-g_ref[...].astype(F32)))
        g_hi = g.astype(BF16)
        g_lo = (g - g_hi.astype(F32)).astype(BF16)
        o = None
        for br, ref in enumerate((oc_ref, os_ref, ow_ref)):
            ge = (jnp.dot(g_hi, ex_ref[br], preferred_element_type=F32)
                  + jnp.dot(g_lo, ex_ref[br], preferred_element_type=F32))
            term = ge * ref[...].astype(F32)
            o = term if o is None else o + term
        a = o * _silu(z_ref[...].astype(F32))
    else:
        u_ref, bg_ref, c_ref, z_ref, uh_ref, ch_ref, cw_ref = (next(it) for _ in range(7))
        v = c_ref[...].astype(F32) * u_ref[...].astype(F32)
        first = (pl.program_id(0) * tm) % seq == 0
        vh = ch_ref[...].astype(F32) * uh_ref[...].astype(F32)
        vh = jnp.where(first, 0.0, vh)
        h1, h2 = vh[7:8, :], vh[6:7, :]
        row = lax.broadcasted_iota(jnp.int32, v.shape, 0)
        v1 = jnp.where(row == 0, h1, pltpu.roll(v, 1, 0))
        v2 = jnp.where(row == 0, h2, jnp.where(row == 1, h1, pltpu.roll(v, 2, 0)))
        y = cw_ref[0:1, :] * v2 + cw_ref[1:2, :] * v1 + cw_ref[2:3, :] * v
        a = bg_ref[...].astype(F32) * y * _silu(z_ref[...].astype(F32))
    w_ref = next(it)
    fw_ref = next(it) if final_norm else None
    out_ref = next(it)
    xn = x_ref[...] + jnp.dot(a.astype(BF16), w_ref[...], preferred_element_type=F32)
    if final_norm:
        ms = jnp.mean(xn * xn, axis=-1, keepdims=True)
        xn = xn * lax.rsqrt(ms + NORM_EPS) * fw_ref[...]
    out_ref[...] = xn


def _out_proj(x, w_out, mode, inputs, seq, final_w=None):
    T, D = x.shape
    tm = OUT_TM
    assert T % tm == 0 and seq % tm == 0
    row = lambda c: pl.BlockSpec((tm, D), lambda i, c=c: (i, c))
    in_specs, args = [row(0)], [x]
    if mode == "a":
        o, proj = inputs
        in_specs += [row(0), row(1)]
        args += [o, proj]
    elif mode == "b":
        oc, osl, ow, proj, gate_col, expand = inputs
        in_specs += [row(0), row(0), row(0), row(1),
                     pl.BlockSpec((tm, LANES), lambda i: (i, gate_col)),
                     pl.BlockSpec(expand.shape, lambda i: (0, 0, 0))]
        args += [oc, osl, ow, proj, proj, expand]
    else:
        proj, conv_w = inputs
        halo = lambda c: pl.BlockSpec((8, D), lambda i, c=c: (jnp.maximum(i * (tm // 8) - 1, 0), c))
        in_specs += [row(0), row(1), row(2), row(3), halo(0), halo(2),
                     pl.BlockSpec(conv_w.shape, lambda i: (0, 0))]
        args += [proj, proj, proj, proj, proj, proj, conv_w]
    in_specs.append(pl.BlockSpec((D, D), lambda i: (0, 0)))
    args.append(w_out)
    if final_w is not None:
        in_specs.append(pl.BlockSpec((1, D), lambda i: (0, 0)))
        args.append(final_w.reshape(1, D))
    kern = functools.partial(_out_proj_kernel, mode=mode, tm=tm, seq=seq,
                             final_norm=final_w is not None)
    return pl.pallas_call(
        kern,
        grid=(T // tm,),
        in_specs=in_specs,
        out_specs=pl.BlockSpec((tm, D), lambda i: (i, 0)),
        out_shape=jax.ShapeDtypeStruct((T, D), F32),
        compiler_params=_params(("parallel",)),
        name="out_proj_" + mode,
    )(*args)


def _head_perm():
    heads = []
    for pair in range(2):
        for e in range(GROUP):
            for par in range(2):
                heads.append((2 * pair + par) * GROUP + e)
    return np.asarray(heads, np.int32)


_HEADS = _head_perm()
_COLS = (_HEADS[:, None] * HEAD_DIM + np.arange(HEAD_DIM, dtype=np.int32)[None, :]).reshape(-1)
_Q_SCALE = HEAD_DIM ** -0.5 * LOG2E


def _rope_tables(positions):
    inv_freq = ROPE_THETA ** (-jnp.arange(0, ROPE_DIM, 2, dtype=F32) / ROPE_DIM)
    ang = positions.astype(F32).reshape(-1, 1) * inv_freq[None, :]
    cos, sin = jnp.cos(ang), jnp.sin(ang)
    T = ang.shape[0]
    rest = HEAD_DIM - ROPE_DIM
    cos_t = jnp.concatenate([cos, cos, jnp.ones((T, rest), F32)], axis=1)
    sin_a = jnp.concatenate([-sin, jnp.zeros((T, HEAD_DIM - ROPE_HALF), F32)], axis=1)
    sin_b = jnp.concatenate([jnp.zeros((T, ROPE_HALF), F32), sin, jnp.zeros((T, rest), F32)], axis=1)
    return tuple(jnp.concatenate([t, t], axis=1) for t in (cos_t, sin_a, sin_b))


def _overlap_t(seq):
    nc = (seq - CMP_BLOCK) // CMP_STRIDE + 1
    nsel = seq // SLC_BLOCK
    c_start = np.arange(nc) * CMP_STRIDE
    c_end = c_start + CMP_BLOCK
    s_start = np.arange(nsel) * SLC_BLOCK
    s_end = s_start + SLC_BLOCK
    ov = np.clip(np.minimum(c_end[:, None], s_end[None, :]) - np.maximum(c_start[:, None], s_start[None, :]), 0, None)
    ov = (ov / CMP_BLOCK).astype(np.float32)
    ovt = np.zeros((nsel, seq // CMP_STRIDE), np.float32)
    ovt[:, :nc] = ov.T
    return jnp.asarray(ovt, BF16), nc


def _queries_t(proj, batch, seq):
    q = proj[:, :D_MODEL].reshape(batch, seq, 2, GROUP, LANES)
    return q.transpose(0, 2, 3, 4, 1)


def _values_t(proj, col, batch, seq, kb):
    v = proj[:, col:col + 2 * LANES].reshape(batch, seq // kb, kb, 2, LANES)
    return v.transpose(0, 3, 1, 4, 2)


def _mixer_a(x, norm_g, w_in, sinks, w_out, tabs, batch, seq, final_w=None):
    q, k, v, z = (w_in[:, :2048], w_in[:, 2048:2304], w_in[:, 2304:2560], w_in[:, 2560:])
    w = jnp.concatenate([q[:, _COLS] * _Q_SCALE, z[:, _COLS], k, v], axis=1).astype(BF16)
    proj = _norm_proj(x, norm_g, w, tabs, rope_full_tiles=(0, 1, 2, 3), rope_half_tiles=(8,))
    kb = 128
    o = _flash(_queries_t(proj, batch, seq), proj, _values_t(proj, 4352, batch, seq, kb),
               batch, seq, k_col=32, window=A_WINDOW, kb=kb, sinks=sinks)
    return _out_proj(x, w_out[_COLS, :].astype(BF16), "a", (o, proj), seq, final_w)


def _mixer_b(x, norm_g, w_in, kc_pos, kc_w1, kc_w2, vc_pos, vc_w1, vc_w2, w_out, tabs, batch, seq,
             final_w=None):
    c = [2048 + 256 * n for n in range(7)]
    q = w_in[:, :2048]
    kc, vc, ks, vs, kw, vw = (w_in[:, c[n]:c[n + 1]] for n in range(6))
    gates = w_in[:, c[6]:c[6] + 96]
    z = w_in[:, c[6] + 96:]
    gate_cols = np.concatenate([br * N_HEADS + _HEADS for br in range(3)])
    w = jnp.concatenate([q[:, _COLS] * _Q_SCALE, z[:, _COLS], ks, kw, kc, vc, vs, vw,
                         gates[:, gate_cols], jnp.zeros((D_MODEL, PROJ_TN - 96), F32)], axis=1).astype(BF16)
    proj = _norm_proj(x, norm_g, w, tabs, rope_full_tiles=(0, 1, 2, 3, 8))

    chunks = seq // CMP_STRIDE

    def chunked(col):
        t = proj[:, col:col + 256].reshape(batch, chunks, CMP_STRIDE, N_GROUPS, HEAD_DIM)
        return t.transpose(0, 3, 1, 2, 4).reshape(batch, N_GROUPS, chunks, CMP_STRIDE * HEAD_DIM)

    def w2_pair(w2, transposed):
        zero = jnp.zeros_like(w2)
        pair = jnp.stack([jnp.concatenate([w2, zero], axis=1), jnp.concatenate([zero, w2], axis=1)])
        return (pair.transpose(0, 2, 1) if transposed else pair).astype(BF16)

    half = CMP_STRIDE * HEAD_DIM
    kcmp = _compress(chunked(4608), kc_pos.reshape(2, half), kc_w1.astype(BF16),
                     w2_pair(kc_w2, False), False)
    vcmp_t = _compress(chunked(4864), vc_pos.reshape(2, half), vc_w1.astype(BF16),
                       w2_pair(vc_w2, True), True)

    qt = _queries_t(proj, batch, seq)
    ovt, n_cmp = _overlap_t(seq)
    o_cmp, mbt = _cmp_select(qt, kcmp, vcmp_t, ovt, batch, seq, n_cmp)
    e_all = jnp.asarray((np.arange(seq)[:, None] // SLC_BLOCK) == np.arange(LANES)[None, :], BF16)
    kb = 256
    o_slc = _flash(qt, proj, _values_t(proj, 5120, batch, seq, kb), batch, seq, k_col=32,
                   window=seq, kb=kb, mbt=mbt, e_all=e_all)
    o_win = _flash(qt, proj, _values_t(proj, 5376, batch, seq, kb), batch, seq, k_col=34,
                   window=B_WINDOW, kb=kb)

    expand = np.zeros((3, LANES, D_MODEL), np.float32)
    for br in range(3):
        for slot in range(N_HEADS):
            expand[br, br * N_HEADS + slot, slot * HEAD_DIM:(slot + 1) * HEAD_DIM] = 1.0
    return _out_proj(x, w_out[_COLS, :].astype(BF16), "b",
                     (o_cmp, o_slc, o_win, proj, 44, jnp.asarray(expand, BF16)), seq, final_w)


def _mixer_c(x, norm_g, w_in, conv_w, w_out, tabs, seq, final_w=None):
    proj = _norm_proj(x, norm_g, w_in.astype(BF16), tabs)
    cw = jnp.concatenate([conv_w, jnp.zeros((8 - conv_w.shape[0], conv_w.shape[1]), F32)], axis=0)
    return _out_proj(x, w_out.astype(BF16), "c", (proj, cw), seq, final_w)


def kernel(x, positions, norm_w, final_norm_w, a_w_in, a_sinks, a_w_out, b_w_in, b_cmp_k_pos, b_cmp_k_w1,
           b_cmp_k_w2, b_cmp_v_pos, b_cmp_v_w1, b_cmp_v_w2, b_w_out, c_w_in, c_conv_w, c_w_out):
    batch, seq, d = x.shape
    depth = norm_w.shape[0]
    tabs = _rope_tables(positions)
    xf = x.reshape(batch * seq, d)
    for i in range(depth):
        kind, j = i % 3, i // 3
        fw = final_norm_w if i == depth - 1 else None
        if kind == 0:
            xf = _mixer_a(xf, norm_w[i], a_w_in[j], a_sinks[j], a_w_out[j], tabs, batch, seq, fw)
        elif kind == 1:
            xf = _mixer_b(xf, norm_w[i], b_w_in[j], b_cmp_k_pos[j], b_cmp_k_w1[j], b_cmp_k_w2[j],
                          b_cmp_v_pos[j], b_cmp_v_w1[j], b_cmp_v_w2[j], b_w_out[j], tabs, batch, seq, fw)
        else:
            xf = _mixer_c(xf, norm_w[i], c_w_in[j], c_conv_w[j], c_w_out[j], tabs, seq, fw)
    return xf.reshape(batch, seq, d)
```

```python
import functools
import math

import numpy as np
import jax
import jax.numpy as jnp
from jax import lax
from jax.experimental import pallas as pl
from jax.experimental.pallas import tpu as pltpu

F32 = jnp.float32
BF16 = jnp.bfloat16

D_MODEL = 2048
HEAD_DIM = 64
N_HEADS = 32
N_GROUPS = 4
GROUP = 8
ROPE_DIM = 16
ROPE_HALF = 8
ROPE_THETA = 500000.0
NORM_EPS = 1e-5
A_WINDOW = 128
B_WINDOW = 512
CMP_BLOCK = 32
CMP_STRIDE = 16
SLC_BLOCK = 64
SLC_TOPK = 16
CMP_HIDDEN = 256

LANES = 128
NEG = -2.0 ** 100
LOG2E = math.log2(math.e)
VMEM_LIMIT = 56 * 1024 * 1024

PROJ_TM = 1024
PROJ_TN = 512
OUT_TM = 256
ATT_QB = 128
ACC_ROWS = HEAD_DIM + 16


def _params(sem):
    return pltpu.CompilerParams(dimension_semantics=sem, vmem_limit_bytes=VMEM_LIMIT)


def _norm_proj_kernel(x_ref, g_ref, w_ref, cos_ref, sa_ref, sb_ref, o_ref, h_ref, *,
                      rope_full_tiles, rope_half_tiles):
    j = pl.program_id(1)

    @pl.when(j == 0)
    def _():
        x = x_ref[...]
        ms = jnp.mean(x * x, axis=-1, keepdims=True)
        h_ref[...] = (x * lax.rsqrt(ms + NORM_EPS) * g_ref[...]).astype(BF16)

    acc = jnp.dot(h_ref[...], w_ref[...], preferred_element_type=F32)
    tn = acc.shape[1]

    def rope(a):
        n = a.shape[1]
        reps = n // LANES
        cos = jnp.concatenate([cos_ref[...]] * reps, axis=1)
        sa = jnp.concatenate([sa_ref[...]] * reps, axis=1)
        sb = jnp.concatenate([sb_ref[...]] * reps, axis=1)
        return a * cos + pltpu.roll(a, n - ROPE_HALF, 1) * sa + pltpu.roll(a, ROPE_HALF, 1) * sb

    def any_of(tiles):
        c = j == tiles[0]
        for t in tiles[1:]:
            c = c | (j == t)
        return c

    plain = None
    if rope_full_tiles:
        full = any_of(rope_full_tiles)
        plain = ~full

        @pl.when(full)
        def _():
            o_ref[...] = rope(acc).astype(o_ref.dtype)

    if rope_half_tiles:
        half = any_of(rope_half_tiles)
        plain = ~half if plain is None else plain & ~half

        @pl.when(half)
        def _():
            hw = tn // 2
            o_ref[:, :hw] = rope(acc[:, :hw]).astype(o_ref.dtype)
            o_ref[:, hw:] = acc[:, hw:].astype(o_ref.dtype)

    if plain is None:
        o_ref[...] = acc.astype(o_ref.dtype)
    else:
        @pl.when(plain)
        def _():
            o_ref[...] = acc.astype(o_ref.dtype)


def _norm_proj(x, g, w, tabs, rope_full_tiles=(), rope_half_tiles=()):
    T, D = x.shape
    N = w.shape[1]
    tm, tn = min(PROJ_TM, T), PROJ_TN
    assert T % tm == 0 and N % tn == 0
    kern = functools.partial(_norm_proj_kernel, rope_full_tiles=tuple(rope_full_tiles),
                             rope_half_tiles=tuple(rope_half_tiles))
    tab_spec = pl.BlockSpec((tm, LANES), lambda i, j: (i, 0))
    return pl.pallas_call(
        kern,
        grid=(T // tm, N // tn),
        in_specs=[pl.BlockSpec((tm, D), lambda i, j: (i, 0)),
                  pl.BlockSpec((1, D), lambda i, j: (0, 0)),
                  pl.BlockSpec((D, tn), lambda i, j: (0, j)),
                  tab_spec, tab_spec, tab_spec],
        out_specs=pl.BlockSpec((tm, tn), lambda i, j: (i, j)),
        out_shape=jax.ShapeDtypeStruct((T, N), BF16),
        scratch_shapes=[pltpu.VMEM((tm, D), BF16)],
        compiler_params=_params(("parallel", "arbitrary")),
        name="norm_proj",
    )(x, g.reshape(1, D), w, *tabs)


def _placed_queries(qt_ref, hp, par, qb):
    feat = lax.broadcasted_iota(jnp.int32, (LANES, qb), 0)
    own = (feat >= HEAD_DIM) if par else (feat < HEAD_DIM)
    cols = []
    for e in (2 * hp, 2 * hp + 1):
        t = qt_ref[0, 0, e]
        cols.append(jnp.where(own, t, jnp.zeros_like(t)))
    return cols


def _gate_row(gt_ref, par, hp):
    g = 1.0 / (1.0 + jnp.exp(-gt_ref[0, 0, 0].astype(F32)))
    r = par * GROUP + 2 * hp
    return jnp.concatenate([g[r:r + 1], g[r + 1:r + 2]], axis=1)


def _store_token_major(o_ref, get_ot, qb):
    eye = (lax.broadcasted_iota(jnp.int32, (qb, qb), 0)
           == lax.broadcasted_iota(jnp.int32, (qb, qb), 1)).astype(BF16)
    for e in range(GROUP):
        hp, sub = divmod(e, 2)
        ot = jnp.concatenate([get_ot(par, hp)[:, sub * qb:(sub + 1) * qb] for par in (0, 1)],
                             axis=0).astype(BF16)
        o = lax.dot_general(eye, ot, (((1,), (1,)), ((), ())), preferred_element_type=F32)
        o_ref[:, e * LANES:(e + 1) * LANES] = o.astype(o_ref.dtype)


def _flash_kernel(*refs, seq, qb, kb, window, use_sinks, use_sel, use_gate):
    it = iter(refs)
    sinks_ref = next(it) if use_sinks else None
    qt_ref, k_ref, vt_ref = next(it), next(it), next(it)
    mbt_ref = next(it) if use_sel else None
    e_ref = next(it) if use_sel else None
    gt_ref = next(it) if use_gate else None
    o_ref, qa_scr, m_scr, acc_scr, st_scr, p_scr, al_scr = (next(it) for _ in range(7))

    pp = pl.program_id(1)
    i = pl.program_id(2)
    q0 = i * qb
    ch = 2 * qb
    n_chain = 2 * (GROUP // 2)

    half = (GROUP // 2) * ch
    lane = lax.broadcasted_iota(jnp.int32, (1, ch), 1)
    acc_row = lax.broadcasted_iota(jnp.int32, (ACC_ROWS, ch), 0)
    for c in range(n_chain):
        par, hp = divmod(c, GROUP // 2)
        cols = _placed_queries(qt_ref, hp, par, qb)
        if use_sel:
            cols = [jnp.concatenate([t, mbt_ref[0, par]], axis=0) for t in cols]
        qa_scr[:, c * ch:(c + 1) * ch] = jnp.concatenate(cols, axis=1)
        if use_sinks:
            head = (2 * pp + par) * GROUP + 2 * hp
            sk = jnp.where(lane < qb, sinks_ref[head], sinks_ref[head + 1]) * LOG2E
            m0 = sk.astype(BF16).astype(F32)
            m_scr[:, c * ch:(c + 1) * ch] = m0
            asl = slice(hp * ch, (hp + 1) * ch)
            acc_scr[par, :, asl] = jnp.where(acc_row == HEAD_DIM, jnp.exp2(sk - m0), 0.0)
    if not use_sinks:
        m_scr[...] = jnp.full((1, 2 * half), NEG, F32)
        acc_scr[...] = jnp.zeros((2, ACC_ROWS, half), F32)

    q_minus_k = (lax.broadcasted_iota(jnp.int32, (kb, qb), 1)
                 - lax.broadcasted_iota(jnp.int32, (kb, qb), 0))
    ones_rows = jnp.where(lax.broadcasted_iota(jnp.int32, (ACC_ROWS - HEAD_DIM, kb), 0) == 0,
                          1.0, 0.0).astype(BF16)
    last = (q0 + qb - 1) // kb
    first = jnp.maximum(q0 - (window - 1), 0) // kb if window < seq else 0
    mask_all = window < seq

    def scores(kt, buf):
        k0 = pl.multiple_of(kt * kb, kb)
        ka = k_ref[pl.ds(k0, kb), :]
        if use_sel:
            ka = jnp.concatenate([ka, e_ref[pl.ds(k0, kb), :]], axis=1)
        for c in range(n_chain):
            sl = slice(c * ch, (c + 1) * ch)
            st_scr[buf, :, sl] = jnp.dot(ka, qa_scr[:, sl],
                                         preferred_element_type=F32).astype(BF16)

    def update(kt, buf, masked):
        k0 = pl.multiple_of(kt * kb, kb)
        vt = vt_ref[0, 0, kt]
        if masked:
            d = q_minus_k + (q0 - k0)
            vis = d >= 0
            if window < seq:
                vis = vis & (d < window)
            bias = jnp.where(vis, 0.0, NEG).astype(BF16)
        for c in range(n_chain):
            par, hp = divmod(c, GROUP // 2)
            for col in range(c * ch, (c + 1) * ch, qb):
                sl = slice(col, col + qb)
                x = st_scr[buf, :, sl]
                if masked:
                    x = x + bias
                m_prev = m_scr[:, sl]
                m_new = jnp.maximum(m_prev, jnp.max(x, axis=0, keepdims=True).astype(F32))
                al_scr[:, sl] = jnp.exp2(m_prev - m_new)
                m_scr[:, sl] = m_new
                p_scr[:, sl] = jnp.exp2(x - m_new.astype(BF16))
            sl = slice(c * ch, (c + 1) * ch)
            vta = jnp.concatenate([vt[par * HEAD_DIM:(par + 1) * HEAD_DIM], ones_rows], axis=0)
            pv = jnp.dot(vta, p_scr[:, sl], preferred_element_type=F32)
            asl = slice(hp * ch, (hp + 1) * ch)
            acc_scr[par, :, asl] = acc_scr[par, :, asl] * al_scr[:, sl] + pv

    scores(first, 0)
    n_pairs = (last - first) // 2

    def body(j, carry):
        kt = first + 2 * j
        scores(kt + 1, 1)
        update(kt, 0, mask_all)
        scores(kt + 2, 0)
        update(kt + 1, 1, mask_all)
        return carry

    lax.fori_loop(0, n_pairs, body, 0)
    odd = lax.rem(last - first, 2) == 1

    @pl.when(odd)
    def _():
        scores(last, 1)
        update(last - 1, 0, mask_all)
        update(last, 1, True)

    @pl.when(jnp.logical_not(odd))
    def _():
        update(last, 0, True)

    def get_ot(par, hp):
        acc = acc_scr[par][:, hp * ch:(hp + 1) * ch]
        o = acc[:HEAD_DIM] * (1.0 / acc[HEAD_DIM:HEAD_DIM + 1])
        return o * _gate_row(gt_ref, par, hp) if use_gate else o

    _store_token_major(o_ref, get_ot, qb)


def _flash(qt, proj, vt, batch, seq, k_col, window, kb, sinks=None, mbt=None, e_all=None, gate=None):
    T = proj.shape[0]
    qb = ATT_QB
    nq = seq // qb
    ch = 2 * qb
    use_sinks, use_sel, use_gate = sinks is not None, mbt is not None, gate is not None
    kern = functools.partial(_flash_kernel, seq=seq, qb=qb, kb=kb, window=window,
                             use_sinks=use_sinks, use_sel=use_sel, use_gate=use_gate)
    in_specs, args = [], []
    if use_sinks:
        in_specs.append(pl.BlockSpec(memory_space=pltpu.SMEM))
        args.append(sinks)
    in_specs += [pl.BlockSpec((1, 1, GROUP, LANES, qb), lambda b, p, i: (b, p, 0, 0, i)),
                 pl.BlockSpec((seq, LANES), lambda b, p, i: (b, k_col + p)),
                 pl.BlockSpec((1, 1, seq // kb, LANES, kb), lambda b, p, i: (b, p, 0, 0, 0))]
    args += [qt, proj, vt]
    if use_sel:
        in_specs += [pl.BlockSpec((1, 2, LANES, qb), lambda b, p, i: (b, p, 0, i)),
                     pl.BlockSpec((seq, LANES), lambda b, p, i: (0, 0))]
        args += [mbt, e_all]
    if use_gate:
        gt, br = gate
        in_specs.append(pl.BlockSpec((1, 1, 1, 2 * GROUP, qb), lambda b, p, i: (b, br, p, 0, i)))
        args.append(gt)
    n_chain = GROUP
    return pl.pallas_call(
        kern,
        grid=(batch, 2, nq),
        in_specs=in_specs,
        out_specs=pl.BlockSpec((qb, GROUP * LANES), lambda b, p, i: (b * nq + i, p)),
        out_shape=jax.ShapeDtypeStruct((T, D_MODEL), BF16),
        scratch_shapes=[pltpu.VMEM((2 * LANES if use_sel else LANES, n_chain * ch), BF16),
                        pltpu.VMEM((1, n_chain * ch), F32),
                        pltpu.VMEM((2, ACC_ROWS, n_chain * ch // 2), F32),
                        pltpu.VMEM((2, kb, n_chain * ch), BF16),
                        pltpu.VMEM((kb, n_chain * ch), BF16),
                        pltpu.VMEM((1, n_chain * ch), F32)],
        compiler_params=_params(("parallel", "parallel", "arbitrary")),
        name="flash_sel" if use_sel else ("flash_sink" if use_sinks else "flash_win"),
    )(*args)


def _gelu_tanh(x):
    return 0.5 * x * (1.0 + jnp.tanh(math.sqrt(2.0 / math.pi) * (x + 0.044715 * (x * x * x))))


def _compress_kernel(u_ref, pos_ref, w1_ref, w2_ref, o_ref, *, transposed):
    out = None
    half = w1_ref.shape[0] // 2
    for par in (0, 1):
        u = u_ref[0, par].astype(F32)
        lo = (u + pos_ref[0:1, :]).astype(BF16)
        hi = (u + pos_ref[1:2, :]).astype(BF16)
        a = jnp.dot(lo, w1_ref[:half, :], preferred_element_type=F32)
        b = jnp.dot(hi, w1_ref[half:, :], preferred_element_type=F32)
        b_next = jnp.concatenate([b[1:], jnp.zeros((1, b.shape[1]), F32)], axis=0)
        hid = _gelu_tanh(a + b_next).astype(BF16)
        if transposed:
            term = lax.dot_general(w2_ref[par], hid, (((1,), (1,)), ((), ())),
                                   preferred_element_type=F32)
        else:
            term = jnp.dot(hid, w2_ref[par], preferred_element_type=F32)
        out = term if out is None else out + term
    o_ref[0, 0] = out.astype(o_ref.dtype)


def _compress(u, pos, w1, w2_pair, transposed):
    batch, _, chunks, feat = u.shape
    oshape = (LANES, chunks) if transposed else (chunks, LANES)
    return pl.pallas_call(
        functools.partial(_compress_kernel, transposed=transposed),
        grid=(batch, 2),
        in_specs=[pl.BlockSpec((1, 2, chunks, feat), lambda b, p: (b, p, 0, 0)),
                  pl.BlockSpec((2, feat), lambda b, p: (0, 0)),
                  pl.BlockSpec((2 * feat, CMP_HIDDEN), lambda b, p: (0, 0)),
                  pl.BlockSpec(w2_pair.shape, lambda b, p: (0, 0, 0))],
        out_specs=pl.BlockSpec((1, 1) + oshape, lambda b, p: (b, p, 0, 0)),
        out_shape=jax.ShapeDtypeStruct((batch, 2) + oshape, BF16),
        compiler_params=_params(("parallel", "parallel")),
        name="compress",
    )(u, pos, w1, w2_pair)


def _cmp_select_kernel(qt_ref, kc_ref, vct_ref, ovt_ref, gt_ref, o_ref, mbt_ref, ot_scr, *,
                       qb, n_cmp, n_sel, n_top):
    i = pl.program_id(2)
    q0 = i * qb
    nck = kc_ref.shape[2]
    ch = 2 * qb
    c_sub = lax.broadcasted_iota(jnp.int32, (nck, ch), 0)
    t_lane = q0 + lax.rem(lax.broadcasted_iota(jnp.int32, (nck, ch), 1), qb)
    valid = (c_sub * CMP_STRIDE + (CMP_BLOCK - 1) <= t_lane) & (c_sub < n_cmp)
    blk = lax.broadcasted_iota(jnp.int32, (n_sel, qb), 0)
    cur = (q0 + lax.broadcasted_iota(jnp.int32, (n_sel, qb), 1)) // SLC_BLOCK
    kc = kc_ref[0, 0]

    for par in (0, 1):
        vt = vct_ref[0, 0][par * HEAD_DIM:(par + 1) * HEAD_DIM]
        p_sum = jnp.zeros((nck, qb), F32)
        for hp in range(GROUP // 2):
            qa = jnp.concatenate(_placed_queries(qt_ref, hp, par, qb), axis=1)
            st = jnp.dot(kc, qa, preferred_element_type=F32)
            st = jnp.where(valid, st, NEG)
            m = jnp.max(st, axis=0, keepdims=True)
            ex = jnp.where(valid, jnp.exp2(st - m), 0.0)
            p = ex * (1.0 / jnp.maximum(jnp.sum(ex, axis=0, keepdims=True), 1e-30))
            ot_scr[par, hp] = jnp.dot(vt, p.astype(BF16), preferred_element_type=F32)
            p_sum = p_sum + p[:, :qb] + p[:, qb:]

        imp_t = jnp.zeros((n_sel, qb), F32)
        rest = p_sum
        for _ in range(3):
            piece = rest.astype(BF16)
            rest = rest - piece.astype(F32)
            imp_t = imp_t + jnp.dot(ovt_ref[...], piece, preferred_element_type=F32)
        forced = (blk == 0) | (blk == cur) | (blk == cur - 1)
        imp_t = jnp.where(forced, jnp.inf, imp_t)
        imp_t = jnp.where(blk > cur, -jnp.inf, imp_t)
        rank = jnp.zeros((n_sel, qb), jnp.int32)
        for r in range(n_sel):
            row = imp_t[r:r + 1, :]
            before = (row > imp_t) | ((row == imp_t) & (blk > r))
            rank = rank + before.astype(jnp.int32)
        bias_t = jnp.where(rank < n_top, 0.0, NEG)
        mbt_ref[0, par] = jnp.concatenate(
            [bias_t, jnp.zeros((LANES - n_sel, qb), F32)], axis=0).astype(mbt_ref.dtype)

    _store_token_major(o_ref, lambda par, hp: ot_scr[par, hp] * _gate_row(gt_ref, par, hp), qb)


def _cmp_select(qt, kcmp, vcmp_t, ovt, gt, batch, seq, n_cmp):
    qb = ATT_QB
    nq = seq // qb
    nck = kcmp.shape[2]
    n_sel = seq // SLC_BLOCK
    assert n_sel <= LANES
    kern = functools.partial(_cmp_select_kernel, qb=qb, n_cmp=n_cmp, n_sel=n_sel,
                             n_top=min(SLC_TOPK, n_sel))
    return pl.pallas_call(
        kern,
        grid=(batch, 2, nq),
        in_specs=[pl.BlockSpec((1, 1, GROUP, LANES, qb), lambda b, p, i: (b, p, 0, 0, i)),
                  pl.BlockSpec((1, 1, nck, LANES), lambda b, p, i: (b, p, 0, 0)),
                  pl.BlockSpec((1, 1, LANES, nck), lambda b, p, i: (b, p, 0, 0)),
                  pl.BlockSpec((n_sel, nck), lambda b, p, i: (0, 0)),
                  pl.BlockSpec((1, 1, 1, 2 * GROUP, qb), lambda b, p, i: (b, 0, p, 0, i))],
        out_specs=[pl.BlockSpec((qb, GROUP * LANES), lambda b, p, i: (b * nq + i, p)),
                   pl.BlockSpec((1, 2, LANES, qb), lambda b, p, i: (b, p, 0, i))],
        out_shape=[jax.ShapeDtypeStruct((batch * seq, D_MODEL), BF16),
                   jax.ShapeDtypeStruct((batch, N_GROUPS, LANES, seq), BF16)],
        scratch_shapes=[pltpu.VMEM((2, GROUP // 2, HEAD_DIM, 2 * qb), F32)],
        compiler_params=_params(("parallel", "parallel", "arbitrary")),
        name="cmp_select",
    )(qt, kcmp, vcmp_t, ovt, gt)


def _silu(z):
    return z / (1.0 + jnp.exp(-z))


def _out_proj_kernel(*refs, mode, tm, seq, final_norm):
    it = iter(refs)
    x_ref = next(it)
    if mode == "a":
        o_ref, z_ref = next(it), next(it)
        a = o_ref[...].astype(F32) * _silu(z_ref[...].astype(F32))
    elif mode == "b":
        oc_ref, os_ref, ow_ref, z_ref = (next(it) for _ in range(4))
        o = oc_ref[...].astype(F32) + os_ref[...].astype(F32) + ow_ref[...].astype(F32)
        a = o * _silu(z_ref[...].astype(F32))
    else:
        u_ref, bg_ref, c_ref, z_ref, uh_ref, ch_ref, cw_ref = (next(it) for _ in range(7))
        v = c_ref[...].astype(F32) * u_ref[...].astype(F32)
        first = (pl.program_id(0) * tm) % seq == 0
        vh = ch_ref[...].astype(F32) * uh_ref[...].astype(F32)
        vh = jnp.where(first, 0.0, vh)
        h1, h2 = vh[7:8, :], vh[6:7, :]
        row = lax.broadcasted_iota(jnp.int32, v.shape, 0)
        v1 = jnp.where(row == 0, h1, pltpu.roll(v, 1, 0))
        v2 = jnp.where(row == 0, h2, jnp.where(row == 1, h1, pltpu.roll(v, 2, 0)))
        y = cw_ref[0:1, :] * v2 + cw_ref[1:2, :] * v1 + cw_ref[2:3, :] * v
        a = bg_ref[...].astype(F32) * y * _silu(z_ref[...].astype(F32))
    w_ref = next(it)
    fw_ref = next(it) if final_norm else None
    out_ref = next(it)
    xn = x_ref[...] + jnp.dot(a.astype(BF16), w_ref[...], preferred_element_type=F32)
    if final_norm:
        ms = jnp.mean(xn * xn, axis=-1, keepdims=True)
        xn = xn * lax.rsqrt(ms + NORM_EPS) * fw_ref[...]
    out_ref[...] = xn


def _out_proj(x, w_out, mode, inputs, seq, final_w=None):
    T, D = x.shape
    tm = OUT_TM
    assert T % tm == 0 and seq % tm == 0
    row = lambda c: pl.BlockSpec((tm, D), lambda i, c=c: (i, c))
    in_specs, args = [row(0)], [x]
    if mode == "a":
        o, proj = inputs
        in_specs += [row(0), row(1)]
        args += [o, proj]
    elif mode == "b":
        oc, osl, ow, proj = inputs
        in_specs += [row(0), row(0), row(0), row(1)]
        args += [oc, osl, ow, proj]
    else:
        proj, conv_w = inputs
        halo = lambda c: pl.BlockSpec((8, D), lambda i, c=c: (jnp.maximum(i * (tm // 8) - 1, 0), c))
        in_specs += [row(0), row(1), row(2), row(3), halo(0), halo(2),
                     pl.BlockSpec(conv_w.shape, lambda i: (0, 0))]
        args += [proj, proj, proj, proj, proj, proj, conv_w]
    in_specs.append(pl.BlockSpec((D, D), lambda i: (0, 0)))
    args.append(w_out)
    if final_w is not None:
        in_specs.append(pl.BlockSpec((1, D), lambda i: (0, 0)))
        args.append(final_w.reshape(1, D))
    kern = functools.partial(_out_proj_kernel, mode=mode, tm=tm, seq=seq,
                             final_norm=final_w is not None)
    return pl.pallas_call(
        kern,
        grid=(T // tm,),
        in_specs=in_specs,
        out_specs=pl.BlockSpec((tm, D), lambda i: (i, 0)),
        out_shape=jax.ShapeDtypeStruct((T, D), F32),
        compiler_params=_params(("parallel",)),
        name="out_proj_" + mode,
    )(*args)


def _permute_cols(w):
    return w.reshape(w.shape[0], 2, 2, GROUP, HEAD_DIM).transpose(0, 1, 3, 2, 4).reshape(w.shape)


def _permute_rows(w):
    return w.reshape(2, 2, GROUP, HEAD_DIM, w.shape[1]).transpose(0, 2, 1, 3, 4).reshape(w.shape)


_Q_SCALE = HEAD_DIM ** -0.5 * LOG2E


def _rope_tables(positions):
    inv_freq = ROPE_THETA ** (-jnp.arange(0, ROPE_DIM, 2, dtype=F32) / ROPE_DIM)
    ang = positions.astype(F32).reshape(-1, 1) * inv_freq[None, :]
    cos, sin = jnp.cos(ang), jnp.sin(ang)
    T = ang.shape[0]
    rest = HEAD_DIM - ROPE_DIM
    cos_t = jnp.concatenate([cos, cos, jnp.ones((T, rest), F32)], axis=1)
    sin_a = jnp.concatenate([-sin, jnp.zeros((T, HEAD_DIM - ROPE_HALF), F32)], axis=1)
    sin_b = jnp.concatenate([jnp.zeros((T, ROPE_HALF), F32), sin, jnp.zeros((T, rest), F32)], axis=1)
    return tuple(jnp.concatenate([t, t], axis=1) for t in (cos_t, sin_a, sin_b))


def _overlap_t(seq):
    nc = (seq - CMP_BLOCK) // CMP_STRIDE + 1
    nsel = seq // SLC_BLOCK
    c_start = np.arange(nc) * CMP_STRIDE
    c_end = c_start + CMP_BLOCK
    s_start = np.arange(nsel) * SLC_BLOCK
    s_end = s_start + SLC_BLOCK
    ov = np.clip(np.minimum(c_end[:, None], s_end[None, :]) - np.maximum(c_start[:, None], s_start[None, :]), 0, None)
    ov = (ov / CMP_BLOCK).astype(np.float32)
    ovt = np.zeros((nsel, seq // CMP_STRIDE), np.float32)
    ovt[:, :nc] = ov.T
    return jnp.asarray(ovt, BF16), nc


def _queries_t(proj, batch, seq):
    q = proj[:, :D_MODEL].reshape(batch, seq, 2, GROUP, LANES)
    return q.transpose(0, 2, 3, 4, 1)


def _values_t(proj, col, batch, seq, kb):
    v = proj[:, col:col + 2 * LANES].reshape(batch, seq // kb, kb, 2, LANES)
    return v.transpose(0, 3, 1, 4, 2)


def _mixer_a(x, norm_g, w_in, sinks, w_out, tabs, batch, seq, final_w=None):
    q, k, v, z = (w_in[:, :2048], w_in[:, 2048:2304], w_in[:, 2304:2560], w_in[:, 2560:])
    w = jnp.concatenate([_permute_cols(q) * _Q_SCALE, _permute_cols(z), k, v], axis=1).astype(BF16)
    proj = _norm_proj(x, norm_g, w, tabs, rope_full_tiles=(0, 1, 2, 3), rope_half_tiles=(8,))
    kb = 128
    o = _flash(_queries_t(proj, batch, seq), proj, _values_t(proj, 4352, batch, seq, kb),
               batch, seq, k_col=32, window=A_WINDOW, kb=kb, sinks=sinks)
    return _out_proj(x, _permute_rows(w_out).astype(BF16), "a", (o, proj), seq, final_w)


def _mixer_b(x, norm_g, w_in, kc_pos, kc_w1, kc_w2, vc_pos, vc_w1, vc_w2, w_out, tabs, batch, seq,
             final_w=None):
    c = [2048 + 256 * n for n in range(7)]
    q = w_in[:, :2048]
    kc, vc, ks, vs, kw, vw = (w_in[:, c[n]:c[n + 1]] for n in range(6))
    gates = w_in[:, c[6]:c[6] + 96]
    z = w_in[:, c[6] + 96:]
    w = jnp.concatenate([_permute_cols(q) * _Q_SCALE, _permute_cols(z), ks, kw, kc, vc, vs, vw,
                         gates, jnp.zeros((D_MODEL, PROJ_TN - 96), F32)], axis=1).astype(BF16)
    proj = _norm_proj(x, norm_g, w, tabs, rope_full_tiles=(0, 1, 2, 3, 8))

    chunks = seq // CMP_STRIDE

    def chunked(col):
        t = proj[:, col:col + 256].reshape(batch, chunks, CMP_STRIDE, N_GROUPS, HEAD_DIM)
        return t.transpose(0, 3, 1, 2, 4).reshape(batch, N_GROUPS, chunks, CMP_STRIDE * HEAD_DIM)

    def w2_pair(w2, transposed):
        zero = jnp.zeros_like(w2)
        pair = jnp.stack([jnp.concatenate([w2, zero], axis=1), jnp.concatenate([zero, w2], axis=1)])
        return (pair.transpose(0, 2, 1) if transposed else pair).astype(BF16)

    half = CMP_STRIDE * HEAD_DIM
    kcmp = _compress(chunked(4608), kc_pos.reshape(2, half), kc_w1.astype(BF16),
                     w2_pair(kc_w2, False), False)
    vcmp_t = _compress(chunked(4864), vc_pos.reshape(2, half), vc_w1.astype(BF16),
                       w2_pair(vc_w2, True), True)

    qt = _queries_t(proj, batch, seq)
    ovt, n_cmp = _overlap_t(seq)
    gt = proj[:, 5632:5632 + 3 * N_HEADS].reshape(batch, seq, 3, 2, 2 * GROUP).transpose(0, 2, 3, 4, 1)
    o_cmp, mbt = _cmp_select(qt, kcmp, vcmp_t, ovt, gt, batch, seq, n_cmp)
    e_all = jnp.asarray((np.arange(seq)[:, None] // SLC_BLOCK) == np.arange(LANES)[None, :], BF16)
    kb = 256
    o_slc = _flash(qt, proj, _values_t(proj, 5120, batch, seq, kb), batch, seq, k_col=32,
                   window=seq, kb=kb, mbt=mbt, e_all=e_all, gate=(gt, 1))
    o_win = _flash(qt, proj, _values_t(proj, 5376, batch, seq, kb), batch, seq, k_col=34,
                   window=B_WINDOW, kb=kb, gate=(gt, 2))

    return _out_proj(x, _permute_rows(w_out).astype(BF16), "b", (o_cmp, o_slc, o_win, proj), seq, final_w)


def _mixer_c(x, norm_g, w_in, conv_w, w_out, tabs, seq, final_w=None):
    proj = _norm_proj(x, norm_g, w_in.astype(BF16), tabs)
    cw = jnp.concatenate([conv_w, jnp.zeros((8 - conv_w.shape[0], conv_w.shape[1]), F32)], axis=0)
    return _out_proj(x, w_out.astype(BF16), "c", (proj, cw), seq, final_w)


def kernel(x, positions, norm_w, final_norm_w, a_w_in, a_sinks, a_w_out, b_w_in, b_cmp_k_pos, b_cmp_k_w1,
           b_cmp_k_w2, b_cmp_v_pos, b_cmp_v_w1, b_cmp_v_w2, b_w_out, c_w_in, c_conv_w, c_w_out):
    batch, seq, d = x.shape
    depth = norm_w.shape[0]
    tabs = _rope_tables(positions)
    xf = x.reshape(batch * seq, d)
    for i in range(depth):
        kind, j = i % 3, i // 3
        fw = final_norm_w if i == depth - 1 else None
        if kind == 0:
            xf = _mixer_a(xf, norm_w[i], a_w_in[j], a_sinks[j], a_w_out[j], tabs, batch, seq, fw)
        elif kind == 1:
            xf = _mixer_b(xf, norm_w[i], b_w_in[j], b_cmp_k_pos[j], b_cmp_k_w1[j], b_cmp_k_w2[j],
                          b_cmp_v_pos[j], b_cmp_v_w1[j], b_cmp_v_w2[j], b_w_out[j], tabs, batch, seq, fw)
        else:
            xf = _mixer_c(xf, norm_w[i], c_w_in[j], c_conv_w[j], c_w_out[j], tabs, seq, fw)
    return xf.reshape(batch, seq, d)
```

```python
import functools
import math

import numpy as np
import jax
import jax.numpy as jnp
from jax import lax
from jax.experimental import pallas as pl
from jax.experimental.pallas import tpu as pltpu

F32 = jnp.float32
BF16 = jnp.bfloat16

D_MODEL = 2048
HEAD_DIM = 64
N_HEADS = 32
N_GROUPS = 4
GROUP = 8
ROPE_DIM = 16
ROPE_HALF = 8
ROPE_THETA = 500000.0
NORM_EPS = 1e-5
A_WINDOW = 128
B_WINDOW = 512
CMP_BLOCK = 32
CMP_STRIDE = 16
SLC_BLOCK = 64
SLC_TOPK = 16
CMP_HIDDEN = 256

LANES = 128
BF16_ROWS = 16
NEG = -2.0 ** 100
LOG2E = math.log2(math.e)
VMEM_LIMIT = 56 * 1024 * 1024

PROJ_TM = 1024
PROJ_TN = 512
OUT_TM = 256
ATT_QB = 128
ACC_ROWS = HEAD_DIM + 16


def _params(sem):
    return pltpu.CompilerParams(dimension_semantics=sem, vmem_limit_bytes=VMEM_LIMIT)


def _norm_proj_kernel(x_ref, g_ref, w_ref, cos_ref, sa_ref, sb_ref, o_ref, h_ref, *,
                      rope_full_tiles, rope_half_tiles):
    j = pl.program_id(1)

    @pl.when(j == 0)
    def _():
        x = x_ref[...]
        ms = jnp.mean(x * x, axis=-1, keepdims=True)
        h_ref[...] = (x * lax.rsqrt(ms + NORM_EPS) * g_ref[...]).astype(BF16)

    acc = jnp.dot(h_ref[...], w_ref[...], preferred_element_type=F32)
    tn = acc.shape[1]

    def rope(a):
        n = a.shape[1]
        reps = n // LANES
        cos = jnp.concatenate([cos_ref[...]] * reps, axis=1)
        sa = jnp.concatenate([sa_ref[...]] * reps, axis=1)
        sb = jnp.concatenate([sb_ref[...]] * reps, axis=1)
        return a * cos + pltpu.roll(a, n - ROPE_HALF, 1) * sa + pltpu.roll(a, ROPE_HALF, 1) * sb

    def any_of(tiles):
        c = j == tiles[0]
        for t in tiles[1:]:
            c = c | (j == t)
        return c

    plain = None
    if rope_full_tiles:
        full = any_of(rope_full_tiles)
        plain = ~full

        @pl.when(full)
        def _():
            o_ref[...] = rope(acc).astype(o_ref.dtype)

    if rope_half_tiles:
        half = any_of(rope_half_tiles)
        plain = ~half if plain is None else plain & ~half

        @pl.when(half)
        def _():
            hw = tn // 2
            o_ref[:, :hw] = rope(acc[:, :hw]).astype(o_ref.dtype)
            o_ref[:, hw:] = acc[:, hw:].astype(o_ref.dtype)

    if plain is None:
        o_ref[...] = acc.astype(o_ref.dtype)
    else:
        @pl.when(plain)
        def _():
            o_ref[...] = acc.astype(o_ref.dtype)


def _norm_proj(x, g, w, tabs, rope_full_tiles=(), rope_half_tiles=()):
    T, D = x.shape
    N = w.shape[1]
    tm, tn = min(PROJ_TM, T), PROJ_TN
    assert T % tm == 0 and N % tn == 0
    kern = functools.partial(_norm_proj_kernel, rope_full_tiles=tuple(rope_full_tiles),
                             rope_half_tiles=tuple(rope_half_tiles))
    tab_spec = pl.BlockSpec((tm, LANES), lambda i, j: (i, 0))
    return pl.pallas_call(
        kern,
        grid=(T // tm, N // tn),
        in_specs=[pl.BlockSpec((tm, D), lambda i, j: (i, 0)),
                  pl.BlockSpec((1, D), lambda i, j: (0, 0)),
                  pl.BlockSpec((D, tn), lambda i, j: (0, j)),
                  tab_spec, tab_spec, tab_spec],
        out_specs=pl.BlockSpec((tm, tn), lambda i, j: (i, j)),
        out_shape=jax.ShapeDtypeStruct((T, N), BF16),
        scratch_shapes=[pltpu.VMEM((tm, D), BF16)],
        compiler_params=_params(("parallel", "arbitrary")),
        name="norm_proj",
    )(x, g.reshape(1, D), w, *tabs)


def _placed_queries(qt_ref, hp, par, qb):
    feat = lax.broadcasted_iota(jnp.int32, (LANES, qb), 0)
    own = (feat >= HEAD_DIM) if par else (feat < HEAD_DIM)
    cols = []
    for e in (2 * hp, 2 * hp + 1):
        t = qt_ref[0, 0, e]
        cols.append(jnp.where(own, t, jnp.zeros_like(t)))
    return cols


def _gate_row(gt_ref, par, hp):
    g = 1.0 / (1.0 + jnp.exp(-gt_ref[0, 0, 0].astype(F32)))
    r = par * GROUP + 2 * hp
    return jnp.concatenate([g[r:r + 1], g[r + 1:r + 2]], axis=1)


def _tile_loop(lo, hi, fn, unroll):
    n_blocks = (hi - lo) // unroll

    def block(j, carry):
        fn([lo + j * unroll + u for u in range(unroll)])
        return carry

    def single(kt, carry):
        fn([kt])
        return carry

    lax.fori_loop(0, n_blocks, block, 0)
    lax.fori_loop(lo + n_blocks * unroll, hi, single, 0)


def _store_token_major(o_ref, get_ot, qb):
    eye = (lax.broadcasted_iota(jnp.int32, (qb, qb), 0)
           == lax.broadcasted_iota(jnp.int32, (qb, qb), 1)).astype(BF16)
    for e in range(GROUP):
        hp, sub = divmod(e, 2)
        ot = jnp.concatenate([get_ot(par, hp)[:, sub * qb:(sub + 1) * qb] for par in (0, 1)],
                             axis=0).astype(BF16)
        o = lax.dot_general(eye, ot, (((1,), (1,)), ((), ())), preferred_element_type=F32)
        o_ref[:, e * LANES:(e + 1) * LANES] = o.astype(o_ref.dtype)


def _flash_kernel(*refs, seq, qb, kb, window, use_sinks, use_sel, use_gate):
    it = iter(refs)
    sinks_ref = next(it) if use_sinks else None
    qt_ref, k_ref, vt_ref = next(it), next(it), next(it)
    mbt_ref = next(it) if use_sel else None
    e_ref = next(it) if use_sel else None
    gt_ref = next(it) if use_gate else None
    o_ref, qa_scr, mx_scr, acc_scr, st_scr = (next(it) for _ in range(5))

    pp = pl.program_id(1)
    i = pl.program_id(2)
    q0 = i * qb
    ch = 2 * qb
    n_chain = 2 * (GROUP // 2)
    half = (GROUP // 2) * ch
    pack = mx_scr.shape[0]

    for c in range(n_chain):
        par, hp = divmod(c, GROUP // 2)
        cols = _placed_queries(qt_ref, hp, par, qb)
        if use_sel:
            cols = [jnp.concatenate([t, mbt_ref[0, par]], axis=0) for t in cols]
        qa_scr[:, c * ch:(c + 1) * ch] = jnp.concatenate(cols, axis=1)
    mx_scr[...] = jnp.full(mx_scr.shape, NEG, BF16)

    q_minus_k = (lax.broadcasted_iota(jnp.int32, (kb, qb), 1)
                 - lax.broadcasted_iota(jnp.int32, (kb, qb), 0))
    ones_rows = jnp.where(lax.broadcasted_iota(jnp.int32, (ACC_ROWS - HEAD_DIM, kb), 0) == 0,
                          1.0, 0.0).astype(BF16)
    last = (q0 + qb - 1) // kb
    first = jnp.maximum(q0 - (window - 1), 0) // kb if window < seq else 0
    unroll = 2 if window < seq else 4

    def scores(kt, masked):
        k0 = pl.multiple_of(kt * kb, kb)
        ka = k_ref[pl.ds(k0, kb), :]
        if use_sel:
            ka = jnp.concatenate([ka, e_ref[pl.ds(k0, kb), :]], axis=1)
        if masked:
            d = q_minus_k + (q0 - k0)
            vis = d >= 0
            if window < seq:
                vis = vis & (d < window)
            bias = jnp.where(vis, 0.0, NEG)
            bias = jnp.concatenate([bias, bias], axis=1)
        for c in range(n_chain):
            sl = slice(c * ch, (c + 1) * ch)
            st = jnp.dot(ka, qa_scr[:, sl], preferred_element_type=F32)
            st = (st + bias if masked else st).astype(BF16)
            st_scr[kt - first, :, sl] = st
            mx_scr[:, sl] = jnp.maximum(mx_scr[:, sl],
                                        jnp.max(st.reshape(kb // pack, pack, ch), axis=0))

    if window < seq:
        _tile_loop(first, last + 1, lambda kts: [scores(kt, True) for kt in kts], unroll)
    else:
        _tile_loop(first, last, lambda kts: [scores(kt, False) for kt in kts], unroll)
        scores(last, True)

    m = jnp.max(mx_scr[...], axis=0, keepdims=True).astype(F32)
    acc_row = lax.broadcasted_iota(jnp.int32, (ACC_ROWS, ch), 0)
    lane = lax.broadcasted_iota(jnp.int32, (1, ch), 1)
    m_cols = []
    for c in range(n_chain):
        par, hp = divmod(c, GROUP // 2)
        asl = slice(hp * ch, (hp + 1) * ch)
        mc = m[:, c * ch:(c + 1) * ch]
        if use_sinks:
            head = (2 * pp + par) * GROUP + 2 * hp
            sk = jnp.where(lane < qb, sinks_ref[head], sinks_ref[head + 1]) * LOG2E
            mc = jnp.maximum(mc, sk).astype(BF16).astype(F32)
            acc_scr[par, :, asl] = jnp.where(acc_row == HEAD_DIM, jnp.exp2(sk - mc), 0.0)
        else:
            acc_scr[par, :, asl] = jnp.zeros((ACC_ROWS, ch), F32)
        m_cols.append(mc.astype(BF16))

    def values(kts):
        vts = [vt_ref[0, 0, kt] for kt in kts]
        for c in range(n_chain):
            par, hp = divmod(c, GROUP // 2)
            p = jnp.concatenate([jnp.exp2(st_scr[kt - first, :, c * ch:(c + 1) * ch] - m_cols[c])
                                 for kt in kts], axis=0)
            vta = jnp.concatenate(
                [jnp.concatenate([vt[par * HEAD_DIM:(par + 1) * HEAD_DIM], ones_rows], axis=0)
                 for vt in vts], axis=1)
            asl = slice(hp * ch, (hp + 1) * ch)
            acc_scr[par, :, asl] += jnp.dot(vta, p, preferred_element_type=F32)

    _tile_loop(first, last + 1, values, unroll)

    def get_ot(par, hp):
        acc = acc_scr[par][:, hp * ch:(hp + 1) * ch]
        o = acc[:HEAD_DIM] * (1.0 / acc[HEAD_DIM:HEAD_DIM + 1])
        return o * _gate_row(gt_ref, par, hp) if use_gate else o

    _store_token_major(o_ref, get_ot, qb)


def _flash(qt, proj, vt, batch, seq, k_col, window, kb, sinks=None, mbt=None, e_all=None, gate=None):
    T = proj.shape[0]
    qb = ATT_QB
    nq = seq // qb
    ch = 2 * qb
    n_chain = GROUP
    n_stage = min(seq // kb, (window + qb - 2) // kb + 2)
    use_sinks, use_sel, use_gate = sinks is not None, mbt is not None, gate is not None
    kern = functools.partial(_flash_kernel, seq=seq, qb=qb, kb=kb, window=window,
                             use_sinks=use_sinks, use_sel=use_sel, use_gate=use_gate)
    in_specs, args = [], []
    if use_sinks:
        in_specs.append(pl.BlockSpec(memory_space=pltpu.SMEM))
        args.append(sinks)
    in_specs += [pl.BlockSpec((1, 1, GROUP, LANES, qb), lambda b, p, i: (b, p, 0, 0, i)),
                 pl.BlockSpec((seq, LANES), lambda b, p, i: (b, k_col + p)),
                 pl.BlockSpec((1, 1, seq // kb, LANES, kb), lambda b, p, i: (b, p, 0, 0, 0))]
    args += [qt, proj, vt]
    if use_sel:
        in_specs += [pl.BlockSpec((1, 2, LANES, qb), lambda b, p, i: (b, p, 0, i)),
                     pl.BlockSpec((seq, LANES), lambda b, p, i: (0, 0))]
        args += [mbt, e_all]
    if use_gate:
        gt, br = gate
        in_specs.append(pl.BlockSpec((1, 1, 1, 2 * GROUP, qb), lambda b, p, i: (b, br, p, 0, i)))
        args.append(gt)
    return pl.pallas_call(
        kern,
        grid=(batch, 2, nq),
        in_specs=in_specs,
        out_specs=pl.BlockSpec((qb, GROUP * LANES), lambda b, p, i: (b * nq + i, p)),
        out_shape=jax.ShapeDtypeStruct((T, D_MODEL), BF16),
        scratch_shapes=[pltpu.VMEM((2 * LANES if use_sel else LANES, n_chain * ch), BF16),
                        pltpu.VMEM((BF16_ROWS, n_chain * ch), BF16),
                        pltpu.VMEM((2, ACC_ROWS, n_chain * ch // 2), F32),
                        pltpu.VMEM((n_stage, kb, n_chain * ch), BF16)],
        compiler_params=_params(("parallel", "parallel", "arbitrary")),
        name="flash_sel" if use_sel else ("flash_sink" if use_sinks else "flash_win"),
    )(*args)


def _gelu_tanh(x):
    return 0.5 * x * (1.0 + jnp.tanh(math.sqrt(2.0 / math.pi) * (x + 0.044715 * (x * x * x))))


def _compress_kernel(u_ref, pos_ref, w1_ref, w2_ref, o_ref, *, transposed):
    out = None
    half = w1_ref.shape[0] // 2
    for par in (0, 1):
        u = u_ref[0, par].astype(F32)
        lo = (u + pos_ref[0:1, :]).astype(BF16)
        hi = (u + pos_ref[1:2, :]).astype(BF16)
        a = jnp.dot(lo, w1_ref[:half, :], preferred_element_type=F32)
        b = jnp.dot(hi, w1_ref[half:, :], preferred_element_type=F32)
        b_next = jnp.concatenate([b[1:], jnp.zeros((1, b.shape[1]), F32)], axis=0)
        hid = _gelu_tanh(a + b_next).astype(BF16)
        if transposed:
            term = lax.dot_general(w2_ref[par], hid, (((1,), (1,)), ((), ())),
                                   preferred_element_type=F32)
        else:
            term = jnp.dot(hid, w2_ref[par], preferred_element_type=F32)
        out = term if out is None else out + term
    o_ref[0, 0] = out.astype(o_ref.dtype)


def _compress(u, pos, w1, w2_pair, transposed):
    batch, _, chunks, feat = u.shape
    oshape = (LANES, chunks) if transposed else (chunks, LANES)
    return pl.pallas_call(
        functools.partial(_compress_kernel, transposed=transposed),
        grid=(batch, 2),
        in_specs=[pl.BlockSpec((1, 2, chunks, feat), lambda b, p: (b, p, 0, 0)),
                  pl.BlockSpec((2, feat), lambda b, p: (0, 0)),
                  pl.BlockSpec((2 * feat, CMP_HIDDEN), lambda b, p: (0, 0)),
                  pl.BlockSpec(w2_pair.shape, lambda b, p: (0, 0, 0))],
        out_specs=pl.BlockSpec((1, 1) + oshape, lambda b, p: (b, p, 0, 0)),
        out_shape=jax.ShapeDtypeStruct((batch, 2) + oshape, BF16),
        compiler_params=_params(("parallel", "parallel")),
        name="compress",
    )(u, pos, w1, w2_pair)


def _cmp_select_kernel(qt_ref, kc_ref, vct_ref, ovt_ref, gt_ref, o_ref, mbt_ref, ot_scr, *,
                       qb, n_cmp, n_sel, n_top):
    i = pl.program_id(2)
    q0 = i * qb
    nck = kc_ref.shape[2]
    ch = 2 * qb
    c_sub = lax.broadcasted_iota(jnp.int32, (nck, ch), 0)
    t_lane = q0 + lax.rem(lax.broadcasted_iota(jnp.int32, (nck, ch), 1), qb)
    valid = (c_sub * CMP_STRIDE + (CMP_BLOCK - 1) <= t_lane) & (c_sub < n_cmp)
    blk = lax.broadcasted_iota(jnp.int32, (n_sel, qb), 0)
    cur = (q0 + lax.broadcasted_iota(jnp.int32, (n_sel, qb), 1)) // SLC_BLOCK
    kc = kc_ref[0, 0]

    for par in (0, 1):
        vt = vct_ref[0, 0][par * HEAD_DIM:(par + 1) * HEAD_DIM]
        p_sum = jnp.zeros((nck, qb), F32)
        for hp in range(GROUP // 2):
            qa = jnp.concatenate(_placed_queries(qt_ref, hp, par, qb), axis=1)
            st = jnp.dot(kc, qa, preferred_element_type=F32)
            st = jnp.where(valid, st, NEG)
            m = jnp.max(st, axis=0, keepdims=True)
            ex = jnp.where(valid, jnp.exp2(st - m), 0.0)
            p = ex * (1.0 / jnp.maximum(jnp.sum(ex, axis=0, keepdims=True), 1e-30))
            ot_scr[par, hp] = jnp.dot(vt, p.astype(BF16), preferred_element_type=F32)
            p_sum = p_sum + p[:, :qb] + p[:, qb:]

        imp_t = jnp.zeros((n_sel, qb), F32)
        rest = p_sum
        for _ in range(3):
            piece = rest.astype(BF16)
            rest = rest - piece.astype(F32)
            imp_t = imp_t + jnp.dot(ovt_ref[...], piece, preferred_element_type=F32)
        forced = (blk == 0) | (blk == cur) | (blk == cur - 1)
        imp_t = jnp.where(forced, jnp.inf, imp_t)
        imp_t = jnp.where(blk > cur, -jnp.inf, imp_t)
        rank = jnp.zeros((n_sel, qb), jnp.int32)
        for r in range(n_sel):
            row = imp_t[r:r + 1, :]
            before = (row > imp_t) | ((row == imp_t) & (blk > r))
            rank = rank + before.astype(jnp.int32)
        bias_t = jnp.where(rank < n_top, 0.0, NEG)
        mbt_ref[0, par] = jnp.concatenate(
            [bias_t, jnp.zeros((LANES - n_sel, qb), F32)], axis=0).astype(mbt_ref.dtype)

    _store_token_major(o_ref, lambda par, hp: ot_scr[par, hp] * _gate_row(gt_ref, par, hp), qb)


def _cmp_select(qt, kcmp, vcmp_t, ovt, gt, batch, seq, n_cmp):
    qb = ATT_QB
    nq = seq // qb
    nck = kcmp.shape[2]
    n_sel = seq // SLC_BLOCK
    assert n_sel <= LANES
    kern = functools.partial(_cmp_select_kernel, qb=qb, n_cmp=n_cmp, n_sel=n_sel,
                             n_top=min(SLC_TOPK, n_sel))
    return pl.pallas_call(
        kern,
        grid=(batch, 2, nq),
        in_specs=[pl.BlockSpec((1, 1, GROUP, LANES, qb), lambda b, p, i: (b, p, 0, 0, i)),
                  pl.BlockSpec((1, 1, nck, LANES), lambda b, p, i: (b, p, 0, 0)),
                  pl.BlockSpec((1, 1, LANES, nck), lambda b, p, i: (b, p, 0, 0)),
                  pl.BlockSpec((n_sel, nck), lambda b, p, i: (0, 0)),
                  pl.BlockSpec((1, 1, 1, 2 * GROUP, qb), lambda b, p, i: (b, 0, p, 0, i))],
        out_specs=[pl.BlockSpec((qb, GROUP * LANES), lambda b, p, i: (b * nq + i, p)),
                   pl.BlockSpec((1, 2, LANES, qb), lambda b, p, i: (b, p, 0, i))],
        out_shape=[jax.ShapeDtypeStruct((batch * seq, D_MODEL), BF16),
                   jax.ShapeDtypeStruct((batch, N_GROUPS, LANES, seq), BF16)],
        scratch_shapes=[pltpu.VMEM((2, GROUP // 2, HEAD_DIM, 2 * qb), F32)],
        compiler_params=_params(("parallel", "parallel", "arbitrary")),
        name="cmp_select",
    )(qt, kcmp, vcmp_t, ovt, gt)


def _silu(z):
    return z / (1.0 + jnp.exp(-z))


def _out_proj_kernel(*refs, mode, tm, seq, final_norm):
    it = iter(refs)
    x_ref = next(it)
    if mode == "a":
        o_ref, z_ref = next(it), next(it)
        a = o_ref[...].astype(F32) * _silu(z_ref[...].astype(F32))
    elif mode == "b":
        oc_ref, os_ref, ow_ref, z_ref = (next(it) for _ in range(4))
        o = oc_ref[...].astype(F32) + os_ref[...].astype(F32) + ow_ref[...].astype(F32)
        a = o * _silu(z_ref[...].astype(F32))
    else:
        u_ref, bg_ref, c_ref, z_ref, uh_ref, ch_ref, cw_ref = (next(it) for _ in range(7))
        v = c_ref[...].astype(F32) * u_ref[...].astype(F32)
        first = (pl.program_id(0) * tm) % seq == 0
        vh = ch_ref[...].astype(F32) * uh_ref[...].astype(F32)
        vh = jnp.where(first, 0.0, vh)
        h1, h2 = vh[7:8, :], vh[6:7, :]
        row = lax.broadcasted_iota(jnp.int32, v.shape, 0)
        v1 = jnp.where(row == 0, h1, pltpu.roll(v, 1, 0))
        v2 = jnp.where(row == 0, h2, jnp.where(row == 1, h1, pltpu.roll(v, 2, 0)))
        y = cw_ref[0:1, :] * v2 + cw_ref[1:2, :] * v1 + cw_ref[2:3, :] * v
        a = bg_ref[...].astype(F32) * y * _silu(z_ref[...].astype(F32))
    w_ref = next(it)
    fw_ref = next(it) if final_norm else None
    out_ref = next(it)
    xn = x_ref[...] + jnp.dot(a.astype(BF16), w_ref[...], preferred_element_type=F32)
    if final_norm:
        ms = jnp.mean(xn * xn, axis=-1, keepdims=True)
        xn = xn * lax.rsqrt(ms + NORM_EPS) * fw_ref[...]
    out_ref[...] = xn


def _out_proj(x, w_out, mode, inputs, seq, final_w=None):
    T, D = x.shape
    tm = OUT_TM
    assert T % tm == 0 and seq % tm == 0
    row = lambda c: pl.BlockSpec((tm, D), lambda i, c=c: (i, c))
    in_specs, args = [row(0)], [x]
    if mode == "a":
        o, proj = inputs
        in_specs += [row(0), row(1)]
        args += [o, proj]
    elif mode == "b":
        oc, osl, ow, proj = inputs
        in_specs += [row(0), row(0), row(0), row(1)]
        args += [oc, osl, ow, proj]
    else:
        proj, conv_w = inputs
        halo = lambda c: pl.BlockSpec((8, D), lambda i, c=c: (jnp.maximum(i * (tm // 8) - 1, 0), c))
        in_specs += [row(0), row(1), row(2), row(3), halo(0), halo(2),
                     pl.BlockSpec(conv_w.shape, lambda i: (0, 0))]
        args += [proj, proj, proj, proj, proj, proj, conv_w]
    in_specs.append(pl.BlockSpec((D, D), lambda i: (0, 0)))
    args.append(w_out)
    if final_w is not None:
        in_specs.append(pl.BlockSpec((1, D), lambda i: (0, 0)))
        args.append(final_w.reshape(1, D))
    kern = functools.partial(_out_proj_kernel, mode=mode, tm=tm, seq=seq,
                             final_norm=final_w is not None)
    return pl.pallas_call(
        kern,
        grid=(T // tm,),
        in_specs=in_specs,
        out_specs=pl.BlockSpec((tm, D), lambda i: (i, 0)),
        out_shape=jax.ShapeDtypeStruct((T, D), F32),
        compiler_params=_params(("parallel",)),
        name="out_proj_" + mode,
    )(*args)


def _permute_cols(w):
    return w.reshape(w.shape[0], 2, 2, GROUP, HEAD_DIM).transpose(0, 1, 3, 2, 4).reshape(w.shape)


def _permute_rows(w):
    return w.reshape(2, 2, GROUP, HEAD_DIM, w.shape[1]).transpose(0, 2, 1, 3, 4).reshape(w.shape)


_Q_SCALE = HEAD_DIM ** -0.5 * LOG2E


def _rope_tables(positions):
    inv_freq = ROPE_THETA ** (-jnp.arange(0, ROPE_DIM, 2, dtype=F32) / ROPE_DIM)
    ang = positions.astype(F32).reshape(-1, 1) * inv_freq[None, :]
    cos, sin = jnp.cos(ang), jnp.sin(ang)
    T = ang.shape[0]
    rest = HEAD_DIM - ROPE_DIM
    cos_t = jnp.concatenate([cos, cos, jnp.ones((T, rest), F32)], axis=1)
    sin_a = jnp.concatenate([-sin, jnp.zeros((T, HEAD_DIM - ROPE_HALF), F32)], axis=1)
    sin_b = jnp.concatenate([jnp.zeros((T, ROPE_HALF), F32), sin, jnp.zeros((T, rest), F32)], axis=1)
    return tuple(jnp.concatenate([t, t], axis=1) for t in (cos_t, sin_a, sin_b))


def _overlap_t(seq):
    nc = (seq - CMP_BLOCK) // CMP_STRIDE + 1
    nsel = seq // SLC_BLOCK
    c_start = np.arange(nc) * CMP_STRIDE
    c_end = c_start + CMP_BLOCK
    s_start = np.arange(nsel) * SLC_BLOCK
    s_end = s_start + SLC_BLOCK
    ov = np.clip(np.minimum(c_end[:, None], s_end[None, :]) - np.maximum(c_start[:, None], s_start[None, :]), 0, None)
    ov = (ov / CMP_BLOCK).astype(np.float32)
    ovt = np.zeros((nsel, seq // CMP_STRIDE), np.float32)
    ovt[:, :nc] = ov.T
    return jnp.asarray(ovt, BF16), nc


def _queries_t(proj, batch, seq):
    q = proj[:, :D_MODEL].reshape(batch, seq, 2, GROUP, LANES)
    return q.transpose(0, 2, 3, 4, 1)


def _values_t(proj, col, batch, seq, kb):
    v = proj[:, col:col + 2 * LANES].reshape(batch, seq // kb, kb, 2, LANES)
    return v.transpose(0, 3, 1, 4, 2)


def _mixer_a(x, norm_g, w_in, sinks, w_out, tabs, batch, seq, final_w=None):
    q, k, v, z = (w_in[:, :2048], w_in[:, 2048:2304], w_in[:, 2304:2560], w_in[:, 2560:])
    w = jnp.concatenate([_permute_cols(q) * _Q_SCALE, _permute_cols(z), k, v], axis=1).astype(BF16)
    proj = _norm_proj(x, norm_g, w, tabs, rope_full_tiles=(0, 1, 2, 3), rope_half_tiles=(8,))
    kb = 128
    o = _flash(_queries_t(proj, batch, seq), proj, _values_t(proj, 4352, batch, seq, kb),
               batch, seq, k_col=32, window=A_WINDOW, kb=kb, sinks=sinks)
    return _out_proj(x, _permute_rows(w_out).astype(BF16), "a", (o, proj), seq, final_w)


def _mixer_b(x, norm_g, w_in, kc_pos, kc_w1, kc_w2, vc_pos, vc_w1, vc_w2, w_out, tabs, batch, seq,
             final_w=None):
    c = [2048 + 256 * n for n in range(7)]
    q = w_in[:, :2048]
    kc, vc, ks, vs, kw, vw = (w_in[:, c[n]:c[n + 1]] for n in range(6))
    gates = w_in[:, c[6]:c[6] + 96]
    z = w_in[:, c[6] + 96:]
    w = jnp.concatenate([_permute_cols(q) * _Q_SCALE, _permute_cols(z), ks, kw, kc, vc, vs, vw,
                         gates, jnp.zeros((D_MODEL, PROJ_TN - 96), F32)], axis=1).astype(BF16)
    proj = _norm_proj(x, norm_g, w, tabs, rope_full_tiles=(0, 1, 2, 3, 8))

    chunks = seq // CMP_STRIDE

    def chunked(col):
        t = proj[:, col:col + 256].reshape(batch, chunks, CMP_STRIDE, N_GROUPS, HEAD_DIM)
        return t.transpose(0, 3, 1, 2, 4).reshape(batch, N_GROUPS, chunks, CMP_STRIDE * HEAD_DIM)

    def w2_pair(w2, transposed):
        zero = jnp.zeros_like(w2)
        pair = jnp.stack([jnp.concatenate([w2, zero], axis=1), jnp.concatenate([zero, w2], axis=1)])
        return (pair.transpose(0, 2, 1) if transposed else pair).astype(BF16)

    half = CMP_STRIDE * HEAD_DIM
    kcmp = _compress(chunked(4608), kc_pos.reshape(2, half), kc_w1.astype(BF16),
                     w2_pair(kc_w2, False), False)
    vcmp_t = _compress(chunked(4864), vc_pos.reshape(2, half), vc_w1.astype(BF16),
                       w2_pair(vc_w2, True), True)

    qt = _queries_t(proj, batch, seq)
    ovt, n_cmp = _overlap_t(seq)
    gt = proj[:, 5632:5632 + 3 * N_HEADS].reshape(batch, seq, 3, 2, 2 * GROUP).transpose(0, 2, 3, 4, 1)
    o_cmp, mbt = _cmp_select(qt, kcmp, vcmp_t, ovt, gt, batch, seq, n_cmp)
    e_all = jnp.asarray((np.arange(seq)[:, None] // SLC_BLOCK) == np.arange(LANES)[None, :], BF16)
    kb = 256
    o_slc = _flash(qt, proj, _values_t(proj, 5120, batch, seq, kb), batch, seq, k_col=32,
                   window=seq, kb=kb, mbt=mbt, e_all=e_all, gate=(gt, 1))
    o_win = _flash(qt, proj, _values_t(proj, 5376, batch, seq, kb), batch, seq, k_col=34,
                   window=B_WINDOW, kb=kb, gate=(gt, 2))

    return _out_proj(x, _permute_rows(w_out).astype(BF16), "b", (o_cmp, o_slc, o_win, proj), seq, final_w)


def _mixer_c(x, norm_g, w_in, conv_w, w_out, tabs, seq, final_w=None):
    proj = _norm_proj(x, norm_g, w_in.astype(BF16), tabs)
    cw = jnp.concatenate([conv_w, jnp.zeros((8 - conv_w.shape[0], conv_w.shape[1]), F32)], axis=0)
    return _out_proj(x, w_out.astype(BF16), "c", (proj, cw), seq, final_w)


def kernel(x, positions, norm_w, final_norm_w, a_w_in, a_sinks, a_w_out, b_w_in, b_cmp_k_pos, b_cmp_k_w1,
           b_cmp_k_w2, b_cmp_v_pos, b_cmp_v_w1, b_cmp_v_w2, b_w_out, c_w_in, c_conv_w, c_w_out):
    batch, seq, d = x.shape
    depth = norm_w.shape[0]
    tabs = _rope_tables(positions)
    xf = x.reshape(batch * seq, d)
    for i in range(depth):
        kind, j = i % 3, i // 3
        fw = final_norm_w if i == depth - 1 else None
        if kind == 0:
            xf = _mixer_a(xf, norm_w[i], a_w_in[j], a_sinks[j], a_w_out[j], tabs, batch, seq, fw)
        elif kind == 1:
            xf = _mixer_b(xf, norm_w[i], b_w_in[j], b_cmp_k_pos[j], b_cmp_k_w1[j], b_cmp_k_w2[j],
                          b_cmp_v_pos[j], b_cmp_v_w1[j], b_cmp_v_w2[j], b_w_out[j], tabs, batch, seq, fw)
        else:
            xf = _mixer_c(xf, norm_w[i], c_w_in[j], c_conv_w[j], c_w_out[j], tabs, seq, fw)
    return xf.reshape(batch, seq, d)
```

```python
import functools
import math

import numpy as np
import jax
import jax.numpy as jnp
from jax import lax
from jax.experimental import pallas as pl
from jax.experimental.pallas import tpu as pltpu

F32 = jnp.float32
BF16 = jnp.bfloat16

D_MODEL = 2048
HEAD_DIM = 64
N_HEADS = 32
N_GROUPS = 4
GROUP = 8
ROPE_DIM = 16
ROPE_HALF = 8
ROPE_THETA = 500000.0
NORM_EPS = 1e-5
A_WINDOW = 128
B_WINDOW = 512
CMP_BLOCK = 32
CMP_STRIDE = 16
SLC_BLOCK = 64
SLC_TOPK = 16
CMP_HIDDEN = 256

LANES = 128
BF16_ROWS = 16
NEG = -2.0 ** 100
LOG2E = math.log2(math.e)
_Q_SCALE = HEAD_DIM ** -0.5 * LOG2E
VMEM_LIMIT = 56 * 1024 * 1024

PROJ_TM = 1024
PROJ_TN = 512
OUT_TM = 256
ATT_QB = 128
ACC_ROWS = HEAD_DIM + 16


def _params(sem):
    return pltpu.CompilerParams(dimension_semantics=sem, vmem_limit_bytes=VMEM_LIMIT)


def _norm_proj_kernel(x_ref, g_ref, w_ref, cos_ref, sa_ref, sb_ref, o_ref, h_ref, *,
                      query_tiles, rope_full_tiles, rope_half_tiles):
    j = pl.program_id(1)

    @pl.when(j == 0)
    def _():
        x = x_ref[...]
        ms = jnp.mean(x * x, axis=-1, keepdims=True)
        h_ref[...] = (x * lax.rsqrt(ms + NORM_EPS) * g_ref[...]).astype(BF16)

    acc = jnp.dot(h_ref[...], w_ref[0].astype(BF16), preferred_element_type=F32)
    tn = acc.shape[1]

    def rope(a):
        n = a.shape[1]
        reps = n // LANES
        cos = jnp.concatenate([cos_ref[...]] * reps, axis=1)
        sa = jnp.concatenate([sa_ref[...]] * reps, axis=1)
        sb = jnp.concatenate([sb_ref[...]] * reps, axis=1)
        return a * cos + pltpu.roll(a, n - ROPE_HALF, 1) * sa + pltpu.roll(a, ROPE_HALF, 1) * sb

    def any_of(tiles):
        c = j == tiles[0]
        for t in tiles[1:]:
            c = c | (j == t)
        return c

    plain = None
    if query_tiles:
        query = any_of(query_tiles)
        plain = ~query

        @pl.when(query)
        def _():
            o_ref[...] = (rope(acc) * _Q_SCALE).astype(o_ref.dtype)

    if rope_full_tiles:
        full = any_of(rope_full_tiles)
        plain = ~full if plain is None else plain & ~full

        @pl.when(full)
        def _():
            o_ref[...] = rope(acc).astype(o_ref.dtype)

    if rope_half_tiles:
        half = any_of(rope_half_tiles)
        plain = ~half if plain is None else plain & ~half

        @pl.when(half)
        def _():
            hw = tn // 2
            o_ref[:, :hw] = rope(acc[:, :hw]).astype(o_ref.dtype)
            o_ref[:, hw:] = acc[:, hw:].astype(o_ref.dtype)

    if plain is None:
        o_ref[...] = acc.astype(o_ref.dtype)
    else:
        @pl.when(plain)
        def _():
            o_ref[...] = acc.astype(o_ref.dtype)


def _norm_proj(x, g, w, layer, tabs, query_tiles=(), rope_full_tiles=(), rope_half_tiles=()):
    T, D = x.shape
    N = w.shape[2]
    tm, tn = min(PROJ_TM, T), PROJ_TN
    assert T % tm == 0 and N % tn == 0
    kern = functools.partial(_norm_proj_kernel, query_tiles=tuple(query_tiles),
                             rope_full_tiles=tuple(rope_full_tiles),
                             rope_half_tiles=tuple(rope_half_tiles))
    tab_spec = pl.BlockSpec((tm, LANES), lambda i, j: (i, 0))
    return pl.pallas_call(
        kern,
        grid=(T // tm, N // tn),
        in_specs=[pl.BlockSpec((tm, D), lambda i, j: (i, 0)),
                  pl.BlockSpec((1, D), lambda i, j: (0, 0)),
                  pl.BlockSpec((1, D, tn), lambda i, j: (layer, 0, j)),
                  tab_spec, tab_spec, tab_spec],
        out_specs=pl.BlockSpec((tm, tn), lambda i, j: (i, j)),
        out_shape=jax.ShapeDtypeStruct((T, N), BF16),
        scratch_shapes=[pltpu.VMEM((tm, D), BF16)],
        compiler_params=_params(("parallel", "arbitrary")),
        name="norm_proj",
    )(x, g.reshape(1, D), w, *tabs)


def _identity(n):
    return (lax.broadcasted_iota(jnp.int32, (n, n), 0)
            == lax.broadcasted_iota(jnp.int32, (n, n), 1)).astype(BF16)


def _queries_t(q_ref):
    eye = _identity(LANES)
    return [lax.dot_general(eye, q_ref[:, e * LANES:(e + 1) * LANES], (((1,), (1,)), ((), ())),
                            preferred_element_type=F32).astype(BF16) for e in range(GROUP)]


def _placed_queries(qts, hp, par):
    t = qts[par * (GROUP // 2) + hp]
    zero = jnp.zeros((HEAD_DIM, t.shape[1]), t.dtype)
    heads = (t[:HEAD_DIM], t[HEAD_DIM:])
    return [jnp.concatenate([zero, h] if par else [h, zero], axis=0) for h in heads]


def _gate_row(gt_ref, par, hp):
    g = 1.0 / (1.0 + jnp.exp(-gt_ref[0, 0, 0].astype(F32)))
    r = par * GROUP + 2 * hp
    return jnp.concatenate([g[r:r + 1], g[r + 1:r + 2]], axis=1)


def _tile_loop(lo, hi, fn, unroll):
    n_blocks = (hi - lo) // unroll

    def block(j, carry):
        fn([lo + j * unroll + u for u in range(unroll)])
        return carry

    def single(kt, carry):
        fn([kt])
        return carry

    lax.fori_loop(0, n_blocks, block, 0)
    lax.fori_loop(lo + n_blocks * unroll, hi, single, 0)


def _store_token_major(o_ref, get_ot, qb):
    eye = _identity(qb)
    for par in (0, 1):
        for hp in range(GROUP // 2):
            ot = get_ot(par, hp)
            ot = jnp.concatenate([ot[:, :qb], ot[:, qb:]], axis=0).astype(BF16)
            o = lax.dot_general(eye, ot, (((1,), (1,)), ((), ())), preferred_element_type=F32)
            tile = par * (GROUP // 2) + hp
            o_ref[:, tile * LANES:(tile + 1) * LANES] = o.astype(o_ref.dtype)


def _flash_kernel(*refs, seq, qb, kb, window, use_sinks, use_sel, use_gate):
    it = iter(refs)
    sinks_ref = next(it) if use_sinks else None
    q_ref, k_ref, vt_ref = next(it), next(it), next(it)
    mbt_ref = next(it) if use_sel else None
    e_ref = next(it) if use_sel else None
    gt_ref = next(it) if use_gate else None
    o_ref, qa_scr, mx_scr, acc_scr, st_scr = (next(it) for _ in range(5))

    pp = pl.program_id(1)
    i = pl.program_id(2)
    q0 = i * qb
    ch = 2 * qb
    n_chain = 2 * (GROUP // 2)
    half = (GROUP // 2) * ch
    pack = mx_scr.shape[0]

    qts = _queries_t(q_ref)
    for c in range(n_chain):
        par, hp = divmod(c, GROUP // 2)
        cols = _placed_queries(qts, hp, par)
        if use_sel:
            cols = [jnp.concatenate([t, mbt_ref[0, par]], axis=0) for t in cols]
        qa_scr[:, c * ch:(c + 1) * ch] = jnp.concatenate(cols, axis=1)
    mx_scr[...] = jnp.full(mx_scr.shape, NEG, BF16)

    q_minus_k = (lax.broadcasted_iota(jnp.int32, (kb, qb), 1)
                 - lax.broadcasted_iota(jnp.int32, (kb, qb), 0))
    ones_rows = jnp.where(lax.broadcasted_iota(jnp.int32, (ACC_ROWS - HEAD_DIM, kb), 0) == 0,
                          1.0, 0.0).astype(BF16)
    acc_row = lax.broadcasted_iota(jnp.int32, (ACC_ROWS, ch), 0)
    lane = lax.broadcasted_iota(jnp.int32, (1, ch), 1)
    last = (q0 + qb - 1) // kb
    first = jnp.maximum(q0 - (window - 1), 0) // kb if window < seq else 0
    banded = window < seq
    unroll = 2 if banded else 4

    def scores(kt, masked, par):
        k0 = pl.multiple_of(kt * kb, kb)
        ka = k_ref[pl.ds(k0, kb), :]
        if use_sel:
            ka = jnp.concatenate([ka, e_ref[pl.ds(k0, kb), :]], axis=1)
        if masked:
            d = q_minus_k + (q0 - k0)
            vis = d >= 0
            if banded:
                vis = vis & (d < window)
            bias = jnp.where(vis, 0.0, NEG)
            bias = jnp.concatenate([bias, bias], axis=1)
        for hp in range(GROUP // 2):
            c = par * (GROUP // 2) + hp
            sl = slice(c * ch, (c + 1) * ch)
            st = jnp.dot(ka, qa_scr[:, sl], preferred_element_type=F32)
            st = (st + bias if masked else st).astype(BF16)
            st_scr[kt - first, :, sl] = st
            mx_scr[:, sl] = jnp.maximum(mx_scr[:, sl],
                                        jnp.max(st.reshape(kb // pack, pack, ch), axis=0))

    def reference(par):
        m = jnp.max(mx_scr[:, par * half:(par + 1) * half], axis=0, keepdims=True).astype(F32)
        refs_ = []
        for hp in range(GROUP // 2):
            asl = slice(hp * ch, (hp + 1) * ch)
            mc = m[:, asl]
            if use_sinks:
                head = (2 * pp + par) * GROUP + 2 * hp
                sk = jnp.where(lane < qb, sinks_ref[head], sinks_ref[head + 1]) * LOG2E
                mc = jnp.maximum(mc, sk).astype(BF16).astype(F32)
                acc_scr[par, :, asl] = jnp.where(acc_row == HEAD_DIM, jnp.exp2(sk - mc), 0.0)
            else:
                acc_scr[par, :, asl] = jnp.zeros((ACC_ROWS, ch), F32)
            refs_.append(mc.astype(BF16))
        return refs_

    def values(kts, par, m_cols):
        vts = [vt_ref[0, 0, kt] for kt in kts]
        for hp in range(GROUP // 2):
            c = par * (GROUP // 2) + hp
            p = jnp.concatenate([jnp.exp2(st_scr[kt - first, :, c * ch:(c + 1) * ch] - m_cols[hp])
                                 for kt in kts], axis=0)
            vta = jnp.concatenate(
                [jnp.concatenate([vt[par * HEAD_DIM:(par + 1) * HEAD_DIM], ones_rows], axis=0)
                 for vt in vts], axis=1)
            asl = slice(hp * ch, (hp + 1) * ch)
            acc_scr[par, :, asl] += jnp.dot(vta, p, preferred_element_type=F32)

    def sweep(fn):
        if banded:
            _tile_loop(first, last + 1, lambda kts: fn(kts, True), unroll)
        else:
            _tile_loop(first, last, lambda kts: fn(kts, False), unroll)
            fn([last], True)

    sweep(lambda kts, masked: [scores(kt, masked, par) for kt in kts for par in (0, 1)])
    m0, m1 = reference(0), reference(1)
    _tile_loop(first, last + 1, lambda kts: (values(kts, 0, m0), values(kts, 1, m1)), unroll)

    def get_ot(par, hp):
        acc = acc_scr[par][:, hp * ch:(hp + 1) * ch]
        o = acc[:HEAD_DIM] * (1.0 / acc[HEAD_DIM:HEAD_DIM + 1])
        return o * _gate_row(gt_ref, par, hp) if use_gate else o

    _store_token_major(o_ref, get_ot, qb)


def _flash(proj, vt, batch, seq, k_col, window, kb, sinks=None, mbt=None, e_all=None, gate=None):
    T = proj.shape[0]
    qb = ATT_QB
    nq = seq // qb
    ch = 2 * qb
    n_chain = GROUP
    n_stage = min(seq // kb, (window + qb - 2) // kb + 2)
    use_sinks, use_sel, use_gate = sinks is not None, mbt is not None, gate is not None
    kern = functools.partial(_flash_kernel, seq=seq, qb=qb, kb=kb, window=window,
                             use_sinks=use_sinks, use_sel=use_sel, use_gate=use_gate)
    in_specs, args = [], []
    if use_sinks:
        in_specs.append(pl.BlockSpec(memory_space=pltpu.SMEM))
        args.append(sinks)
    in_specs += [pl.BlockSpec((qb, GROUP * LANES), lambda b, p, i: (b * nq + i, p)),
                 pl.BlockSpec((seq, LANES), lambda b, p, i: (b, k_col + p)),
                 pl.BlockSpec((1, 1, seq // kb, LANES, kb), lambda b, p, i: (b, p, 0, 0, 0))]
    args += [proj, proj, vt]
    if use_sel:
        in_specs += [pl.BlockSpec((1, 2, LANES, qb), lambda b, p, i: (b, p, 0, i)),
                     pl.BlockSpec((seq, LANES), lambda b, p, i: (0, 0))]
        args += [mbt, e_all]
    if use_gate:
        gt, br = gate
        in_specs.append(pl.BlockSpec((1, 1, 1, 2 * GROUP, qb), lambda b, p, i: (b, br, p, 0, i)))
        args.append(gt)
    return pl.pallas_call(
        kern,
        grid=(batch, 2, nq),
        in_specs=in_specs,
        out_specs=pl.BlockSpec((qb, GROUP * LANES), lambda b, p, i: (b * nq + i, p)),
        out_shape=jax.ShapeDtypeStruct((T, D_MODEL), BF16),
        scratch_shapes=[pltpu.VMEM((2 * LANES if use_sel else LANES, n_chain * ch), BF16),
                        pltpu.VMEM((BF16_ROWS, n_chain * ch), BF16),
                        pltpu.VMEM((2, ACC_ROWS, n_chain * ch // 2), F32),
                        pltpu.VMEM((n_stage, kb, n_chain * ch), BF16)],
        compiler_params=_params(("parallel", "parallel", "arbitrary")),
        name="flash_sel" if use_sel else ("flash_sink" if use_sinks else "flash_win"),
    )(*args)


def _gelu_tanh(x):
    return 0.5 * x * (1.0 + jnp.tanh(math.sqrt(2.0 / math.pi) * (x + 0.044715 * (x * x * x))))


def _compress_kernel(u_ref, pos_ref, w1_ref, w2_ref, o_ref, *, transposed):
    out = None
    half = w1_ref.shape[0] // 2
    for par in (0, 1):
        u = u_ref[0, par].astype(F32)
        lo = (u + pos_ref[0:1, :]).astype(BF16)
        hi = (u + pos_ref[1:2, :]).astype(BF16)
        a = jnp.dot(lo, w1_ref[:half, :], preferred_element_type=F32)
        b = jnp.dot(hi, w1_ref[half:, :], preferred_element_type=F32)
        b_next = jnp.concatenate([b[1:], jnp.zeros((1, b.shape[1]), F32)], axis=0)
        hid = _gelu_tanh(a + b_next).astype(BF16)
        if transposed:
            term = lax.dot_general(w2_ref[par], hid, (((1,), (1,)), ((), ())),
                                   preferred_element_type=F32)
        else:
            term = jnp.dot(hid, w2_ref[par], preferred_element_type=F32)
        out = term if out is None else out + term
    o_ref[0, 0] = out.astype(o_ref.dtype)


def _compress(u, pos, w1, w2_pair, transposed):
    batch, _, chunks, feat = u.shape
    oshape = (LANES, chunks) if transposed else (chunks, LANES)
    return pl.pallas_call(
        functools.partial(_compress_kernel, transposed=transposed),
        grid=(batch, 2),
        in_specs=[pl.BlockSpec((1, 2, chunks, feat), lambda b, p: (b, p, 0, 0)),
                  pl.BlockSpec((2, feat), lambda b, p: (0, 0)),
                  pl.BlockSpec((2 * feat, CMP_HIDDEN), lambda b, p: (0, 0)),
                  pl.BlockSpec(w2_pair.shape, lambda b, p: (0, 0, 0))],
        out_specs=pl.BlockSpec((1, 1) + oshape, lambda b, p: (b, p, 0, 0)),
        out_shape=jax.ShapeDtypeStruct((batch, 2) + oshape, BF16),
        compiler_params=_params(("parallel", "parallel")),
        name="compress",
    )(u, pos, w1, w2_pair)


def _cmp_select_kernel(q_ref, kc_ref, vct_ref, ovt_ref, gt_ref, o_ref, mbt_ref, ot_scr, *,
                       qb, n_cmp, n_sel, n_top):
    i = pl.program_id(2)
    q0 = i * qb
    nck = kc_ref.shape[2]
    ch = 2 * qb
    c_sub = lax.broadcasted_iota(jnp.int32, (nck, ch), 0)
    t_lane = q0 + lax.rem(lax.broadcasted_iota(jnp.int32, (nck, ch), 1), qb)
    valid = (c_sub * CMP_STRIDE + (CMP_BLOCK - 1) <= t_lane) & (c_sub < n_cmp)
    blk = lax.broadcasted_iota(jnp.int32, (n_sel, qb), 0)
    cur = (q0 + lax.broadcasted_iota(jnp.int32, (n_sel, qb), 1)) // SLC_BLOCK
    kc = kc_ref[0, 0]
    qts = _queries_t(q_ref)

    for par in (0, 1):
        vt = vct_ref[0, 0][par * HEAD_DIM:(par + 1) * HEAD_DIM]
        p_sum = jnp.zeros((nck, qb), F32)
        for hp in range(GROUP // 2):
            qa = jnp.concatenate(_placed_queries(qts, hp, par), axis=1)
            st = jnp.dot(kc, qa, preferred_element_type=F32)
            st = jnp.where(valid, st, NEG)
            m = jnp.max(st, axis=0, keepdims=True)
            m = jnp.where(m > NEG, m, 0.0)
            ex = jnp.exp2(st - m)
            p = ex * (1.0 / jnp.maximum(jnp.sum(ex, axis=0, keepdims=True), 1e-30))
            ot_scr[par, hp] = jnp.dot(vt, p.astype(BF16), preferred_element_type=F32)
            p_sum = p_sum + p[:, :qb] + p[:, qb:]

        imp_t = jnp.zeros((n_sel, qb), F32)
        rest = p_sum
        for _ in range(3):
            piece = rest.astype(BF16)
            rest = rest - piece.astype(F32)
            imp_t = imp_t + jnp.dot(ovt_ref[...], piece, preferred_element_type=F32)
        forced = (blk == 0) | (blk == cur) | (blk == cur - 1)
        imp_t = jnp.where(forced, jnp.inf, imp_t)
        imp_t = jnp.where(blk > cur, -jnp.inf, imp_t)
        rank = jnp.zeros((n_sel, qb), jnp.int32)
        for r in range(n_sel):
            row = imp_t[r:r + 1, :]
            before = (row > imp_t) | ((row == imp_t) & (blk > r))
            rank = rank + before.astype(jnp.int32)
        bias_t = jnp.where(rank < n_top, 0.0, NEG)
        mbt_ref[0, par] = jnp.concatenate(
            [bias_t, jnp.zeros((LANES - n_sel, qb), F32)], axis=0).astype(mbt_ref.dtype)

    _store_token_major(o_ref, lambda par, hp: ot_scr[par, hp] * _gate_row(gt_ref, par, hp), qb)


def _cmp_select(proj, kcmp, vcmp_t, ovt, gt, batch, seq, n_cmp):
    qb = ATT_QB
    nq = seq // qb
    nck = kcmp.shape[2]
    n_sel = seq // SLC_BLOCK
    assert n_sel <= LANES
    kern = functools.partial(_cmp_select_kernel, qb=qb, n_cmp=n_cmp, n_sel=n_sel,
                             n_top=min(SLC_TOPK, n_sel))
    return pl.pallas_call(
        kern,
        grid=(batch, 2, nq),
        in_specs=[pl.BlockSpec((qb, GROUP * LANES), lambda b, p, i: (b * nq + i, p)),
                  pl.BlockSpec((1, 1, nck, LANES), lambda b, p, i: (b, p, 0, 0)),
                  pl.BlockSpec((1, 1, LANES, nck), lambda b, p, i: (b, p, 0, 0)),
                  pl.BlockSpec((n_sel, nck), lambda b, p, i: (0, 0)),
                  pl.BlockSpec((1, 1, 1, 2 * GROUP, qb), lambda b, p, i: (b, 0, p, 0, i))],
        out_specs=[pl.BlockSpec((qb, GROUP * LANES), lambda b, p, i: (b * nq + i, p)),
                   pl.BlockSpec((1, 2, LANES, qb), lambda b, p, i: (b, p, 0, i))],
        out_shape=[jax.ShapeDtypeStruct((batch * seq, D_MODEL), BF16),
                   jax.ShapeDtypeStruct((batch, N_GROUPS, LANES, seq), BF16)],
        scratch_shapes=[pltpu.VMEM((2, GROUP // 2, HEAD_DIM, 2 * qb), F32)],
        compiler_params=_params(("parallel", "parallel", "arbitrary")),
        name="cmp_select",
    )(proj, kcmp, vcmp_t, ovt, gt)


def _silu(z):
    return z / (1.0 + jnp.exp(-z))


def _out_proj_kernel(*refs, mode, tm, seq, n_z, final_norm):
    it = iter(refs)
    x_ref = next(it)
    if mode == "a":
        o_ref = next(it)
        z = jnp.concatenate([next(it)[...] for _ in range(n_z)], axis=1)
        a = o_ref[...].astype(F32) * _silu(z.astype(F32))
    elif mode == "b":
        oc_ref, os_ref, ow_ref = (next(it) for _ in range(3))
        z = jnp.concatenate([next(it)[...] for _ in range(n_z)], axis=1)
        o = oc_ref[...].astype(F32) + os_ref[...].astype(F32) + ow_ref[...].astype(F32)
        a = o * _silu(z.astype(F32))
    else:
        u_ref, bg_ref, c_ref, z_ref, uh_ref, ch_ref, cw_ref = (next(it) for _ in range(7))
        v = c_ref[...].astype(F32) * u_ref[...].astype(F32)
        first = (pl.program_id(0) * tm) % seq == 0
        vh = ch_ref[...].astype(F32) * uh_ref[...].astype(F32)
        vh = jnp.where(first, 0.0, vh)
        h1, h2 = vh[7:8, :], vh[6:7, :]
        row = lax.broadcasted_iota(jnp.int32, v.shape, 0)
        v1 = jnp.where(row == 0, h1, pltpu.roll(v, 1, 0))
        v2 = jnp.where(row == 0, h2, jnp.where(row == 1, h1, pltpu.roll(v, 2, 0)))
        y = cw_ref[0:1, :] * v2 + cw_ref[1:2, :] * v1 + cw_ref[2:3, :] * v
        a = bg_ref[...].astype(F32) * y * _silu(z_ref[...].astype(F32))
    w_ref = next(it)
    fw_ref = next(it) if final_norm else None
    out_ref = next(it)
    xn = x_ref[...] + jnp.dot(a.astype(BF16), w_ref[...], preferred_element_type=F32)
    if final_norm:
        ms = jnp.mean(xn * xn, axis=-1, keepdims=True)
        xn = xn * lax.rsqrt(ms + NORM_EPS) * fw_ref[...]
    out_ref[...] = xn


def _out_proj(x, w_out, mode, inputs, seq, final_w=None):
    T, D = x.shape
    tm = OUT_TM
    assert T % tm == 0 and seq % tm == 0
    row = lambda c: pl.BlockSpec((tm, D), lambda i, c=c: (i, c))
    in_specs, args = [row(0)], [x]
    n_z = D // PROJ_TN
    if mode in ("a", "b"):
        *outs, proj, z_col = inputs
        assert z_col % PROJ_TN == 0
        in_specs += [row(0)] * len(outs)
        in_specs += [pl.BlockSpec((tm, PROJ_TN), lambda i, c=z_col // PROJ_TN + n: (i, c))
                     for n in range(n_z)]
        args += outs + [proj] * n_z
    else:
        proj, conv_w = inputs
        halo = lambda c: pl.BlockSpec((8, D), lambda i, c=c: (jnp.maximum(i * (tm // 8) - 1, 0), c))
        in_specs += [row(0), row(1), row(2), row(3), halo(0), halo(2),
                     pl.BlockSpec(conv_w.shape, lambda i: (0, 0))]
        args += [proj, proj, proj, proj, proj, proj, conv_w]
    in_specs.append(pl.BlockSpec((D, D), lambda i: (0, 0)))
    args.append(w_out)
    if final_w is not None:
        in_specs.append(pl.BlockSpec((1, D), lambda i: (0, 0)))
        args.append(final_w.reshape(1, D))
    kern = functools.partial(_out_proj_kernel, mode=mode, tm=tm, seq=seq, n_z=n_z,
                             final_norm=final_w is not None)
    return pl.pallas_call(
        kern,
        grid=(T // tm,),
        in_specs=in_specs,
        out_specs=pl.BlockSpec((tm, D), lambda i: (i, 0)),
        out_shape=jax.ShapeDtypeStruct((T, D), F32),
        compiler_params=_params(("parallel",)),
        name="out_proj_" + mode,
    )(*args)


def _rope_tables(positions):
    inv_freq = ROPE_THETA ** (-jnp.arange(0, ROPE_DIM, 2, dtype=F32) / ROPE_DIM)
    ang = positions.astype(F32).reshape(-1, 1) * inv_freq[None, :]
    cos, sin = jnp.cos(ang), jnp.sin(ang)
    T = ang.shape[0]
    rest = HEAD_DIM - ROPE_DIM
    cos_t = jnp.concatenate([cos, cos, jnp.ones((T, rest), F32)], axis=1)
    sin_a = jnp.concatenate([-sin, jnp.zeros((T, HEAD_DIM - ROPE_HALF), F32)], axis=1)
    sin_b = jnp.concatenate([jnp.zeros((T, ROPE_HALF), F32), sin, jnp.zeros((T, rest), F32)], axis=1)
    return tuple(jnp.concatenate([t, t], axis=1) for t in (cos_t, sin_a, sin_b))


def _overlap_t(seq):
    nc = (seq - CMP_BLOCK) // CMP_STRIDE + 1
    nsel = seq // SLC_BLOCK
    c_start = np.arange(nc) * CMP_STRIDE
    c_end = c_start + CMP_BLOCK
    s_start = np.arange(nsel) * SLC_BLOCK
    s_end = s_start + SLC_BLOCK
    ov = np.clip(np.minimum(c_end[:, None], s_end[None, :]) - np.maximum(c_start[:, None], s_start[None, :]), 0, None)
    ov = (ov / CMP_BLOCK).astype(np.float32)
    ovt = np.zeros((nsel, seq // CMP_STRIDE), np.float32)
    ovt[:, :nc] = ov.T
    return jnp.asarray(ovt, BF16), nc


def _values_t(proj, col, batch, seq, kb):
    v = proj[:, col:col + 2 * LANES].reshape(batch, seq // kb, kb, 2, LANES)
    return v.transpose(0, 3, 1, 4, 2)


def _mixer_a(x, norm_g, w_in, layer, sinks, w_out, tabs, batch, seq, final_w=None):
    proj = _norm_proj(x, norm_g, w_in, layer, tabs, query_tiles=(0, 1, 2, 3), rope_half_tiles=(4,))
    kb = 128
    o = _flash(proj, _values_t(proj, 2304, batch, seq, kb), batch, seq, k_col=16, window=A_WINDOW,
               kb=kb, sinks=sinks)
    return _out_proj(x, w_out.astype(BF16), "a", (o, proj, 2560), seq, final_w)


def _mixer_b(x, norm_g, w_in, kc_pos, kc_w1, kc_w2, vc_pos, vc_w1, vc_w2, w_out, tabs, batch, seq,
             final_w=None):
    c = [2048 + 256 * n for n in range(7)]
    q = w_in[:, :2048]
    kc, vc, ks, vs, kw, vw = (w_in[:, c[n]:c[n + 1]] for n in range(6))
    gates = w_in[:, c[6]:c[6] + 96]
    z = w_in[:, c[6] + 96:]
    w = jnp.concatenate([q, z, ks, kw, kc, vc, vs, vw, gates,
                         jnp.zeros((D_MODEL, PROJ_TN - 96), F32)], axis=1).astype(BF16)
    proj = _norm_proj(x, norm_g, w[None], 0, tabs, query_tiles=(0, 1, 2, 3), rope_full_tiles=(8,))

    chunks = seq // CMP_STRIDE

    def chunked(col):
        t = proj[:, col:col + 256].reshape(batch, chunks, CMP_STRIDE, N_GROUPS, HEAD_DIM)
        return t.transpose(0, 3, 1, 2, 4).reshape(batch, N_GROUPS, chunks, CMP_STRIDE * HEAD_DIM)

    def w2_pair(w2, transposed):
        zero = jnp.zeros_like(w2)
        pair = jnp.stack([jnp.concatenate([w2, zero], axis=1), jnp.concatenate([zero, w2], axis=1)])
        return (pair.transpose(0, 2, 1) if transposed else pair).astype(BF16)

    half = CMP_STRIDE * HEAD_DIM
    kcmp = _compress(chunked(4608), kc_pos.reshape(2, half), kc_w1.astype(BF16),
                     w2_pair(kc_w2, False), False)
    vcmp_t = _compress(chunked(4864), vc_pos.reshape(2, half), vc_w1.astype(BF16),
                       w2_pair(vc_w2, True), True)

    ovt, n_cmp = _overlap_t(seq)
    gt = proj[:, 5632:5632 + 3 * N_HEADS].reshape(batch, seq, 3, 2, 2 * GROUP).transpose(0, 2, 3, 4, 1)
    o_cmp, mbt = _cmp_select(proj, kcmp, vcmp_t, ovt, gt, batch, seq, n_cmp)
    e_all = jnp.asarray((np.arange(seq)[:, None] // SLC_BLOCK) == np.arange(LANES)[None, :], BF16)
    kb = 256
    o_slc = _flash(proj, _values_t(proj, 5120, batch, seq, kb), batch, seq, k_col=32,
                   window=seq, kb=kb, mbt=mbt, e_all=e_all, gate=(gt, 1))
    o_win = _flash(proj, _values_t(proj, 5376, batch, seq, kb), batch, seq, k_col=34,
                   window=B_WINDOW, kb=kb, gate=(gt, 2))

    return _out_proj(x, w_out.astype(BF16), "b", (o_cmp, o_slc, o_win, proj, 2048), seq, final_w)


def _mixer_c(x, norm_g, w_in, layer, conv_w, w_out, tabs, seq, final_w=None):
    proj = _norm_proj(x, norm_g, w_in, layer, tabs)
    cw = jnp.concatenate([conv_w, jnp.zeros((8 - conv_w.shape[0], conv_w.shape[1]), F32)], axis=0)
    return _out_proj(x, w_out.astype(BF16), "c", (proj, cw), seq, final_w)


def kernel(x, positions, norm_w, final_norm_w, a_w_in, a_sinks, a_w_out, b_w_in, b_cmp_k_pos, b_cmp_k_w1,
           b_cmp_k_w2, b_cmp_v_pos, b_cmp_v_w1, b_cmp_v_w2, b_w_out, c_w_in, c_conv_w, c_w_out):
    batch, seq, d = x.shape
    depth = norm_w.shape[0]
    tabs = _rope_tables(positions)
    xf = x.reshape(batch * seq, d)
    for i in range(depth):
        kind, j = i % 3, i // 3
        fw = final_norm_w if i == depth - 1 else None
        if kind == 0:
            xf = _mixer_a(xf, norm_w[i], a_w_in, j, a_sinks[j], a_w_out[j], tabs, batch, seq, fw)
        elif kind == 1:
            xf = _mixer_b(xf, norm_w[i], b_w_in[j], b_cmp_k_pos[j], b_cmp_k_w1[j], b_cmp_k_w2[j],
                          b_cmp_v_pos[j], b_cmp_v_w1[j], b_cmp_v_w2[j], b_w_out[j], tabs, batch, seq, fw)
        else:
            xf = _mixer_c(xf, norm_w[i], c_w_in, j, c_conv_w[j], c_w_out[j], tabs, seq, fw)
    return xf.reshape(batch, seq, d)
```

```python
import functools
import math

import numpy as np
import jax
import jax.numpy as jnp
from jax import lax
from jax.experimental import pallas as pl
from jax.experimental.pallas import tpu as pltpu

F32 = jnp.float32
BF16 = jnp.bfloat16

D_MODEL = 2048
HEAD_DIM = 64
N_HEADS = 32
N_GROUPS = 4
GROUP = 8
ROPE_DIM = 16
ROPE_HALF = 8
ROPE_THETA = 500000.0
NORM_EPS = 1e-5
A_WINDOW = 128
B_WINDOW = 512
CMP_BLOCK = 32
CMP_STRIDE = 16
SLC_BLOCK = 64
SLC_TOPK = 16
CMP_HIDDEN = 256

LANES = 128
BF16_ROWS = 16
F32_ROWS = 8
NEG = -2.0 ** 100
LOG2E = math.log2(math.e)
_Q_SCALE = HEAD_DIM ** -0.5 * LOG2E
VMEM_LIMIT = 56 * 1024 * 1024

PROJ_TM = 1024
PROJ_TN = 512
OUT_TM = 256
ATT_QB = 128
ACC_ROWS = HEAD_DIM + 16


def _params(sem):
    return pltpu.CompilerParams(dimension_semantics=sem, vmem_limit_bytes=VMEM_LIMIT)


def _norm_proj_kernel(x_ref, g_ref, w_ref, cos_ref, sa_ref, sb_ref, o_ref, h_ref, *,
                      query_tiles, rope_full_tiles, rope_half_tiles):
    j = pl.program_id(1)

    @pl.when(j == 0)
    def _():
        x = x_ref[...]
        ms = jnp.mean(x * x, axis=-1, keepdims=True)
        h_ref[...] = (x * lax.rsqrt(ms + NORM_EPS) * g_ref[...]).astype(BF16)

    acc = jnp.dot(h_ref[...], w_ref[0].astype(BF16), preferred_element_type=F32)
    tn = acc.shape[1]

    def rope(a):
        n = a.shape[1]
        reps = n // LANES
        cos = jnp.concatenate([cos_ref[...]] * reps, axis=1)
        sa = jnp.concatenate([sa_ref[...]] * reps, axis=1)
        sb = jnp.concatenate([sb_ref[...]] * reps, axis=1)
        return a * cos + pltpu.roll(a, n - ROPE_HALF, 1) * sa + pltpu.roll(a, ROPE_HALF, 1) * sb

    def any_of(tiles):
        c = j == tiles[0]
        for t in tiles[1:]:
            c = c | (j == t)
        return c

    plain = None
    if query_tiles:
        query = any_of(query_tiles)
        plain = ~query

        @pl.when(query)
        def _():
            o_ref[...] = (rope(acc) * _Q_SCALE).astype(o_ref.dtype)

    if rope_full_tiles:
        full = any_of(rope_full_tiles)
        plain = ~full if plain is None else plain & ~full

        @pl.when(full)
        def _():
            o_ref[...] = rope(acc).astype(o_ref.dtype)

    if rope_half_tiles:
        half = any_of(rope_half_tiles)
        plain = ~half if plain is None else plain & ~half

        @pl.when(half)
        def _():
            hw = tn // 2
            o_ref[:, :hw] = rope(acc[:, :hw]).astype(o_ref.dtype)
            o_ref[:, hw:] = acc[:, hw:].astype(o_ref.dtype)

    if plain is None:
        o_ref[...] = acc.astype(o_ref.dtype)
    else:
        @pl.when(plain)
        def _():
            o_ref[...] = acc.astype(o_ref.dtype)


def _norm_proj(x, g, w, layer, tabs, query_tiles=(), rope_full_tiles=(), rope_half_tiles=()):
    T, D = x.shape
    N = w.shape[2]
    tm, tn = min(PROJ_TM, T), PROJ_TN
    assert T % tm == 0 and N % tn == 0
    kern = functools.partial(_norm_proj_kernel, query_tiles=tuple(query_tiles),
                             rope_full_tiles=tuple(rope_full_tiles),
                             rope_half_tiles=tuple(rope_half_tiles))
    tab_spec = pl.BlockSpec((tm, LANES), lambda i, j: (i, 0))
    return pl.pallas_call(
        kern,
        grid=(T // tm, N // tn),
        in_specs=[pl.BlockSpec((tm, D), lambda i, j: (i, 0)),
                  pl.BlockSpec((1, D), lambda i, j: (0, 0)),
                  pl.BlockSpec((1, D, tn), lambda i, j: (layer, 0, j)),
                  tab_spec, tab_spec, tab_spec],
        out_specs=pl.BlockSpec((tm, tn), lambda i, j: (i, j)),
        out_shape=jax.ShapeDtypeStruct((T, N), BF16),
        scratch_shapes=[pltpu.VMEM((tm, D), BF16)],
        compiler_params=_params(("parallel", "arbitrary")),
        name="norm_proj",
    )(x, g.reshape(1, D), w, *tabs)


def _identity(n):
    return (lax.broadcasted_iota(jnp.int32, (n, n), 0)
            == lax.broadcasted_iota(jnp.int32, (n, n), 1)).astype(BF16)


def _queries_t(q_ref):
    eye = _identity(LANES)
    return [lax.dot_general(eye, q_ref[:, e * LANES:(e + 1) * LANES], (((1,), (1,)), ((), ())),
                            preferred_element_type=F32).astype(BF16) for e in range(GROUP)]


def _placed_queries(qts, hp, par):
    t = qts[par * (GROUP // 2) + hp]
    zero = jnp.zeros((HEAD_DIM, t.shape[1]), t.dtype)
    heads = (t[:HEAD_DIM], t[HEAD_DIM:])
    return [jnp.concatenate([zero, h] if par else [h, zero], axis=0) for h in heads]


def _gate_row(gt_ref, par, hp):
    g = 1.0 / (1.0 + jnp.exp(-gt_ref[0, 0, 0].astype(F32)))
    r = par * GROUP + 2 * hp
    return jnp.concatenate([g[r:r + 1], g[r + 1:r + 2]], axis=1)


def _tile_loop(lo, hi, fn, unroll):
    n_blocks = (hi - lo) // unroll

    def block(j, carry):
        fn([lo + j * unroll + u for u in range(unroll)])
        return carry

    def single(kt, carry):
        fn([kt])
        return carry

    lax.fori_loop(0, n_blocks, block, 0)
    lax.fori_loop(lo + n_blocks * unroll, hi, single, 0)


def _store_token_major(o_ref, get_ot, qb):
    eye = _identity(qb)
    ots = [get_ot(par, hp) for par in (0, 1) for hp in range(GROUP // 2)]
    ots = [jnp.concatenate([ot[:, :qb], ot[:, qb:]], axis=0).astype(BF16) for ot in ots]
    outs = [lax.dot_general(eye, ot, (((1,), (1,)), ((), ())), preferred_element_type=F32)
            for ot in ots]
    for tile, o in enumerate(outs):
        o_ref[:, tile * LANES:(tile + 1) * LANES] = o.astype(o_ref.dtype)


def _flash_kernel(*refs, seq, qb, kb, window, use_sinks, use_sel, use_gate):
    it = iter(refs)
    sinks_ref = next(it) if use_sinks else None
    q_ref, k_ref, vt_ref = next(it), next(it), next(it)
    mbt_ref = next(it) if use_sel else None
    e_ref = next(it) if use_sel else None
    gt_ref = next(it) if use_gate else None
    o_ref, qa_scr, mx_scr, acc_scr, st_scr = (next(it) for _ in range(5))

    pp = pl.program_id(1)
    i = pl.program_id(2)
    q0 = i * qb
    ch = 2 * qb
    n_chain = 2 * (GROUP // 2)
    half = (GROUP // 2) * ch
    pack = mx_scr.shape[0]

    qts = _queries_t(q_ref)
    for c in range(n_chain):
        par, hp = divmod(c, GROUP // 2)
        cols = _placed_queries(qts, hp, par)
        if use_sel:
            cols = [jnp.concatenate([t, mbt_ref[0, par]], axis=0) for t in cols]
        qa_scr[:, c * ch:(c + 1) * ch] = jnp.concatenate(cols, axis=1)
    mx_scr[...] = jnp.full(mx_scr.shape, NEG, BF16)

    q_minus_k = (lax.broadcasted_iota(jnp.int32, (kb, qb), 1)
                 - lax.broadcasted_iota(jnp.int32, (kb, qb), 0))
    ones_rows = jnp.where(lax.broadcasted_iota(jnp.int32, (ACC_ROWS - HEAD_DIM, kb), 0) == 0,
                          1.0, 0.0).astype(BF16)
    acc_row = lax.broadcasted_iota(jnp.int32, (ACC_ROWS, ch), 0)
    lane = lax.broadcasted_iota(jnp.int32, (1, ch), 1)
    last = (q0 + qb - 1) // kb
    first = jnp.maximum(q0 - (window - 1), 0) // kb if window < seq else 0
    banded = window < seq
    unroll = 2 if banded else 4

    def scores(kt, masked, par):
        k0 = pl.multiple_of(kt * kb, kb)
        ka = k_ref[pl.ds(k0, kb), :]
        if use_sel:
            ka = jnp.concatenate([ka, e_ref[pl.ds(k0, kb), :]], axis=1)
        if masked:
            d = q_minus_k + (q0 - k0)
            vis = d >= 0
            if banded:
                vis = vis & (d < window)
            bias = jnp.where(vis, 0.0, NEG)
            bias = jnp.concatenate([bias, bias], axis=1)
        for hp in range(GROUP // 2):
            c = par * (GROUP // 2) + hp
            sl = slice(c * ch, (c + 1) * ch)
            st = jnp.dot(ka, qa_scr[:, sl], preferred_element_type=F32)
            st = (st + bias if masked else st).astype(BF16)
            st_scr[kt - first, :, sl] = st
            mx_scr[:, sl] = jnp.maximum(mx_scr[:, sl],
                                        jnp.max(st.reshape(kb // pack, pack, ch), axis=0))

    def reference(par):
        m = jnp.max(mx_scr[:, par * half:(par + 1) * half], axis=0, keepdims=True).astype(F32)
        refs_ = []
        for hp in range(GROUP // 2):
            asl = slice(hp * ch, (hp + 1) * ch)
            mc = m[:, asl]
            if use_sinks:
                head = (2 * pp + par) * GROUP + 2 * hp
                sk = jnp.where(lane < qb, sinks_ref[head], sinks_ref[head + 1]) * LOG2E
                mc = jnp.maximum(mc, sk).astype(BF16).astype(F32)
                acc_scr[par, :, asl] = jnp.where(acc_row == HEAD_DIM, jnp.exp2(sk - mc), 0.0)
            else:
                acc_scr[par, :, asl] = jnp.zeros((ACC_ROWS, ch), F32)
            refs_.append(mc.astype(BF16))
        return refs_

    def values(kts, par, m_cols):
        vts = [vt_ref[0, 0, kt] for kt in kts]
        for hp in range(GROUP // 2):
            c = par * (GROUP // 2) + hp
            p = jnp.concatenate([jnp.exp2(st_scr[kt - first, :, c * ch:(c + 1) * ch] - m_cols[hp])
                                 for kt in kts], axis=0)
            vta = jnp.concatenate(
                [jnp.concatenate([vt[par * HEAD_DIM:(par + 1) * HEAD_DIM], ones_rows], axis=0)
                 for vt in vts], axis=1)
            asl = slice(hp * ch, (hp + 1) * ch)
            acc_scr[par, :, asl] += jnp.dot(vta, p, preferred_element_type=F32)

    def sweep(fn):
        if banded:
            _tile_loop(first, last + 1, lambda kts: fn(kts, True), unroll)
        else:
            _tile_loop(first, last, lambda kts: fn(kts, False), unroll)
            fn([last], True)

    sweep(lambda kts, masked: [scores(kt, masked, par) for kt in kts for par in (0, 1)])
    m0, m1 = reference(0), reference(1)
    _tile_loop(first, last + 1, lambda kts: (values(kts, 0, m0), values(kts, 1, m1)), unroll)

    def get_ot(par, hp):
        acc = acc_scr[par][:, hp * ch:(hp + 1) * ch]
        o = acc[:HEAD_DIM] * (1.0 / acc[HEAD_DIM:HEAD_DIM + 1])
        return o * _gate_row(gt_ref, par, hp) if use_gate else o

    _store_token_major(o_ref, get_ot, qb)


def _flash(proj, vt, batch, seq, k_col, window, kb, sinks=None, mbt=None, e_all=None, gate=None):
    T = proj.shape[0]
    qb = ATT_QB
    nq = seq // qb
    ch = 2 * qb
    n_chain = GROUP
    n_stage = min(seq // kb, (window + qb - 2) // kb + 2)
    use_sinks, use_sel, use_gate = sinks is not None, mbt is not None, gate is not None
    kern = functools.partial(_flash_kernel, seq=seq, qb=qb, kb=kb, window=window,
                             use_sinks=use_sinks, use_sel=use_sel, use_gate=use_gate)
    in_specs, args = [], []
    if use_sinks:
        in_specs.append(pl.BlockSpec(memory_space=pltpu.SMEM))
        args.append(sinks)
    in_specs += [pl.BlockSpec((qb, GROUP * LANES), lambda b, p, i: (b * nq + i, p)),
                 pl.BlockSpec((seq, LANES), lambda b, p, i: (b, k_col + p)),
                 pl.BlockSpec((1, 1, seq // kb, LANES, kb), lambda b, p, i: (b, p, 0, 0, 0))]
    args += [proj, proj, vt]
    if use_sel:
        in_specs += [pl.BlockSpec((1, 2, LANES, qb), lambda b, p, i: (b, p, 0, i)),
                     pl.BlockSpec((seq, LANES), lambda b, p, i: (0, 0))]
        args += [mbt, e_all]
    if use_gate:
        gt, br = gate
        in_specs.append(pl.BlockSpec((1, 1, 1, 2 * GROUP, qb), lambda b, p, i: (b, br, p, 0, i)))
        args.append(gt)
    return pl.pallas_call(
        kern,
        grid=(batch, 2, nq),
        in_specs=in_specs,
        out_specs=pl.BlockSpec((qb, GROUP * LANES), lambda b, p, i: (b * nq + i, p)),
        out_shape=jax.ShapeDtypeStruct((T, D_MODEL), BF16),
        scratch_shapes=[pltpu.VMEM((2 * LANES if use_sel else LANES, n_chain * ch), BF16),
                        pltpu.VMEM((BF16_ROWS, n_chain * ch), BF16),
                        pltpu.VMEM((2, ACC_ROWS, n_chain * ch // 2), F32),
                        pltpu.VMEM((n_stage, kb, n_chain * ch), BF16)],
        compiler_params=_params(("parallel", "parallel", "arbitrary")),
        name="flash_sel" if use_sel else ("flash_sink" if use_sinks else "flash_win"),
    )(*args)


def _gelu_tanh(x):
    return 0.5 * x * (1.0 + jnp.tanh(math.sqrt(2.0 / math.pi) * (x + 0.044715 * (x * x * x))))


def _compress_kernel(u_ref, pos_ref, w1_ref, w2_ref, o_ref, *, transposed):
    out = None
    half = w1_ref.shape[0] // 2
    for par in (0, 1):
        u = u_ref[0, par].astype(F32)
        lo = (u + pos_ref[0:1, :]).astype(BF16)
        hi = (u + pos_ref[1:2, :]).astype(BF16)
        a = jnp.dot(lo, w1_ref[:half, :], preferred_element_type=F32)
        b = jnp.dot(hi, w1_ref[half:, :], preferred_element_type=F32)
        b_next = jnp.concatenate([b[1:], jnp.zeros((1, b.shape[1]), F32)], axis=0)
        hid = _gelu_tanh(a + b_next).astype(BF16)
        if transposed:
            term = lax.dot_general(w2_ref[par], hid, (((1,), (1,)), ((), ())),
                                   preferred_element_type=F32)
        else:
            term = jnp.dot(hid, w2_ref[par], preferred_element_type=F32)
        out = term if out is None else out + term
    o_ref[0, 0] = out.astype(o_ref.dtype)


def _compress(u, pos, w1, w2_pair, transposed):
    batch, _, chunks, feat = u.shape
    oshape = (LANES, chunks) if transposed else (chunks, LANES)
    return pl.pallas_call(
        functools.partial(_compress_kernel, transposed=transposed),
        grid=(batch, 2),
        in_specs=[pl.BlockSpec((1, 2, chunks, feat), lambda b, p: (b, p, 0, 0)),
                  pl.BlockSpec((2, feat), lambda b, p: (0, 0)),
                  pl.BlockSpec((2 * feat, CMP_HIDDEN), lambda b, p: (0, 0)),
                  pl.BlockSpec(w2_pair.shape, lambda b, p: (0, 0, 0))],
        out_specs=pl.BlockSpec((1, 1) + oshape, lambda b, p: (b, p, 0, 0)),
        out_shape=jax.ShapeDtypeStruct((batch, 2) + oshape, BF16),
        compiler_params=_params(("parallel", "parallel")),
        name="compress",
    )(u, pos, w1, w2_pair)


def _cmp_select_kernel(q_ref, kc_ref, vct_ref, ovt_ref, gt_ref, o_ref, mbt_ref, ot_scr, st_scr, *,
                       qb, n_cmp, n_sel, n_top):
    i = pl.program_id(2)
    q0 = i * qb
    nck = kc_ref.shape[2]
    ch = 2 * qb
    c_sub = lax.broadcasted_iota(jnp.int32, (nck, qb), 0)
    t_lane = q0 + lax.broadcasted_iota(jnp.int32, (nck, qb), 1)
    valid = (c_sub * CMP_STRIDE + (CMP_BLOCK - 1) <= t_lane) & (c_sub < n_cmp)
    blk = lax.broadcasted_iota(jnp.int32, (n_sel, qb), 0)
    cur = (q0 + lax.broadcasted_iota(jnp.int32, (n_sel, qb), 1)) // SLC_BLOCK
    kc = kc_ref[0, 0]
    qts = _queries_t(q_ref)

    for par in (0, 1):
        vt = vct_ref[0, 0][par * HEAD_DIM:(par + 1) * HEAD_DIM]
        heads = range(GROUP)
        qa = jnp.concatenate([c for hp in range(GROUP // 2)
                              for c in _placed_queries(qts, hp, par)], axis=1)
        st_scr[...] = jnp.dot(kc, qa, preferred_element_type=F32)
        ss = [jnp.where(valid, st_scr[:, h * qb:(h + 1) * qb], NEG) for h in heads]
        ms = [jnp.max(s, axis=0, keepdims=True) for s in ss]
        ms = [jnp.where(m > NEG, m, 0.0) for m in ms]
        exs = [jnp.exp2(s - m) for s, m in zip(ss, ms)]
        invs = [1.0 / jnp.maximum(jnp.sum(ex, axis=0, keepdims=True), 1e-30) for ex in exs]
        ps = [ex * inv for ex, inv in zip(exs, invs)]
        p_sum = functools.reduce(lambda a, b: a + b, ps)
        p_all = jnp.concatenate([p.astype(BF16) for p in ps], axis=1)
        o_all = jnp.dot(vt, p_all, preferred_element_type=F32)
        for hp in range(GROUP // 2):
            ot_scr[par, hp] = o_all[:, hp * ch:(hp + 1) * ch]

        imp_t = jnp.zeros((n_sel, qb), F32)
        rest = p_sum
        for _ in range(3):
            piece = rest.astype(BF16)
            rest = rest - piece.astype(F32)
            imp_t = imp_t + jnp.dot(ovt_ref[...], piece, preferred_element_type=F32)
        forced = (blk == 0) | (blk == cur) | (blk == cur - 1)
        imp_t = jnp.where(forced, jnp.inf, imp_t)
        imp_t = jnp.where(blk > cur, -jnp.inf, imp_t)
        groups = [imp_t[g * F32_ROWS:(g + 1) * F32_ROWS] for g in range(n_sel // F32_ROWS)]
        ranks = [jnp.zeros((F32_ROWS, qb), jnp.int32) for _ in groups]
        blk_in_group = lax.broadcasted_iota(jnp.int32, (F32_ROWS, qb), 0)
        for r in range(n_sel):
            row = imp_t[r:r + 1, :]
            for g, x in enumerate(groups):
                if g > r // F32_ROWS:
                    before = row >= x
                elif g < r // F32_ROWS:
                    before = row > x
                else:
                    before = (row > x) | ((row == x) & (blk_in_group > r % F32_ROWS))
                ranks[g] = ranks[g] + before.astype(jnp.int32)
        rank = jnp.concatenate(ranks, axis=0)
        bias_t = jnp.where(rank < n_top, 0.0, NEG)
        mbt_ref[0, par] = jnp.concatenate(
            [bias_t, jnp.zeros((LANES - n_sel, qb), F32)], axis=0).astype(mbt_ref.dtype)

    _store_token_major(o_ref, lambda par, hp: ot_scr[par, hp] * _gate_row(gt_ref, par, hp), qb)


def _cmp_select(proj, kcmp, vcmp_t, ovt, gt, batch, seq, n_cmp):
    qb = ATT_QB
    nq = seq // qb
    nck = kcmp.shape[2]
    n_sel = seq // SLC_BLOCK
    assert n_sel <= LANES
    kern = functools.partial(_cmp_select_kernel, qb=qb, n_cmp=n_cmp, n_sel=n_sel,
                             n_top=min(SLC_TOPK, n_sel))
    return pl.pallas_call(
        kern,
        grid=(batch, 2, nq),
        in_specs=[pl.BlockSpec((qb, GROUP * LANES), lambda b, p, i: (b * nq + i, p)),
                  pl.BlockSpec((1, 1, nck, LANES), lambda b, p, i: (b, p, 0, 0)),
                  pl.BlockSpec((1, 1, LANES, nck), lambda b, p, i: (b, p, 0, 0)),
                  pl.BlockSpec((n_sel, nck), lambda b, p, i: (0, 0)),
                  pl.BlockSpec((1, 1, 1, 2 * GROUP, qb), lambda b, p, i: (b, 0, p, 0, i))],
        out_specs=[pl.BlockSpec((qb, GROUP * LANES), lambda b, p, i: (b * nq + i, p)),
                   pl.BlockSpec((1, 2, LANES, qb), lambda b, p, i: (b, p, 0, i))],
        out_shape=[jax.ShapeDtypeStruct((batch * seq, D_MODEL), BF16),
                   jax.ShapeDtypeStruct((batch, N_GROUPS, LANES, seq), BF16)],
        scratch_shapes=[pltpu.VMEM((2, GROUP // 2, HEAD_DIM, 2 * qb), F32),
                        pltpu.VMEM((nck, GROUP * qb), F32)],
        compiler_params=_params(("parallel", "parallel", "arbitrary")),
        name="cmp_select",
    )(proj, kcmp, vcmp_t, ovt, gt)


def _silu(z):
    return z / (1.0 + jnp.exp(-z))


def _out_proj_kernel(*refs, mode, tm, seq, n_z, final_norm):
    it = iter(refs)
    x_ref = next(it)
    if mode == "a":
        o_ref = next(it)
        z = jnp.concatenate([next(it)[...] for _ in range(n_z)], axis=1)
        a = o_ref[...].astype(F32) * _silu(z.astype(F32))
    elif mode == "b":
        oc_ref, os_ref, ow_ref = (next(it) for _ in range(3))
        z = jnp.concatenate([next(it)[...] for _ in range(n_z)], axis=1)
        o = oc_ref[...].astype(F32) + os_ref[...].astype(F32) + ow_ref[...].astype(F32)
        a = o * _silu(z.astype(F32))
    else:
        u_ref, bg_ref, c_ref, z_ref, uh_ref, ch_ref, cw_ref = (next(it) for _ in range(7))
        v = c_ref[...].astype(F32) * u_ref[...].astype(F32)
        first = (pl.program_id(0) * tm) % seq == 0
        vh = ch_ref[...].astype(F32) * uh_ref[...].astype(F32)
        vh = jnp.where(first, 0.0, vh)
        h1, h2 = vh[7:8, :], vh[6:7, :]
        row = lax.broadcasted_iota(jnp.int32, v.shape, 0)
        v1 = jnp.where(row == 0, h1, pltpu.roll(v, 1, 0))
        v2 = jnp.where(row == 0, h2, jnp.where(row == 1, h1, pltpu.roll(v, 2, 0)))
        y = cw_ref[0:1, :] * v2 + cw_ref[1:2, :] * v1 + cw_ref[2:3, :] * v
        a = bg_ref[...].astype(F32) * y * _silu(z_ref[...].astype(F32))
    w_ref = next(it)
    fw_ref = next(it) if final_norm else None
    out_ref = next(it)
    xn = x_ref[...] + jnp.dot(a.astype(BF16), w_ref[...], preferred_element_type=F32)
    if final_norm:
        ms = jnp.mean(xn * xn, axis=-1, keepdims=True)
        xn = xn * lax.rsqrt(ms + NORM_EPS) * fw_ref[...]
    out_ref[...] = xn


def _out_proj(x, w_out, mode, inputs, seq, final_w=None):
    T, D = x.shape
    tm = OUT_TM
    assert T % tm == 0 and seq % tm == 0
    row = lambda c: pl.BlockSpec((tm, D), lambda i, c=c: (i, c))
    in_specs, args = [row(0)], [x]
    n_z = D // PROJ_TN
    if mode in ("a", "b"):
        *outs, proj, z_col = inputs
        assert z_col % PROJ_TN == 0
        in_specs += [row(0)] * len(outs)
        in_specs += [pl.BlockSpec((tm, PROJ_TN), lambda i, c=z_col // PROJ_TN + n: (i, c))
                     for n in range(n_z)]
        args += outs + [proj] * n_z
    else:
        proj, conv_w = inputs
        halo = lambda c: pl.BlockSpec((8, D), lambda i, c=c: (jnp.maximum(i * (tm // 8) - 1, 0), c))
        in_specs += [row(0), row(1), row(2), row(3), halo(0), halo(2),
                     pl.BlockSpec(conv_w.shape, lambda i: (0, 0))]
        args += [proj, proj, proj, proj, proj, proj, conv_w]
    in_specs.append(pl.BlockSpec((D, D), lambda i: (0, 0)))
    args.append(w_out)
    if final_w is not None:
        in_specs.append(pl.BlockSpec((1, D), lambda i: (0, 0)))
        args.append(final_w.reshape(1, D))
    kern = functools.partial(_out_proj_kernel, mode=mode, tm=tm, seq=seq, n_z=n_z,
                             final_norm=final_w is not None)
    return pl.pallas_call(
        kern,
        grid=(T // tm,),
        in_specs=in_specs,
        out_specs=pl.BlockSpec((tm, D), lambda i: (i, 0)),
        out_shape=jax.ShapeDtypeStruct((T, D), F32),
        compiler_params=_params(("parallel",)),
        name="out_proj_" + mode,
    )(*args)


def _rope_tables(positions):
    inv_freq = ROPE_THETA ** (-jnp.arange(0, ROPE_DIM, 2, dtype=F32) / ROPE_DIM)
    head = jnp.concatenate([inv_freq, inv_freq, jnp.zeros((HEAD_DIM - ROPE_DIM,), F32)])
    ang = positions.astype(F32).reshape(-1, 1) * jnp.concatenate([head, head])[None, :]
    cos, sin = jnp.cos(ang), jnp.sin(ang)
    d = np.arange(LANES) % HEAD_DIM
    sin_a = jnp.where(jnp.asarray(d < ROPE_HALF), -sin, 0.0)
    sin_b = jnp.where(jnp.asarray((d >= ROPE_HALF) & (d < ROPE_DIM)), sin, 0.0)
    return cos, sin_a, sin_b


def _overlap_t(seq):
    nc = (seq - CMP_BLOCK) // CMP_STRIDE + 1
    nsel = seq // SLC_BLOCK
    c_start = np.arange(nc) * CMP_STRIDE
    c_end = c_start + CMP_BLOCK
    s_start = np.arange(nsel) * SLC_BLOCK
    s_end = s_start + SLC_BLOCK
    ov = np.clip(np.minimum(c_end[:, None], s_end[None, :]) - np.maximum(c_start[:, None], s_start[None, :]), 0, None)
    ov = (ov / CMP_BLOCK).astype(np.float32)
    ovt = np.zeros((nsel, seq // CMP_STRIDE), np.float32)
    ovt[:, :nc] = ov.T
    return jnp.asarray(ovt, BF16), nc


def _values_t(proj, col, batch, seq, kb):
    v = proj[:, col:col + 2 * LANES].reshape(batch, seq // kb, kb, 2, LANES)
    return v.transpose(0, 3, 1, 4, 2)


def _mixer_a(x, norm_g, w_in, layer, sinks, w_out, tabs, batch, seq, final_w=None):
    proj = _norm_proj(x, norm_g, w_in, layer, tabs, query_tiles=(0, 1, 2, 3), rope_half_tiles=(4,))
    kb = 128
    o = _flash(proj, _values_t(proj, 2304, batch, seq, kb), batch, seq, k_col=16, window=A_WINDOW,
               kb=kb, sinks=sinks)
    return _out_proj(x, w_out.astype(BF16), "a", (o, proj, 2560), seq, final_w)


def _mixer_b(x, norm_g, w_in, kc_pos, kc_w1, kc_w2, vc_pos, vc_w1, vc_w2, w_out, tabs, batch, seq,
             final_w=None):
    c = [2048 + 256 * n for n in range(7)]
    q = w_in[:, :2048]
    kc, vc, ks, vs, kw, vw = (w_in[:, c[n]:c[n + 1]] for n in range(6))
    gates = w_in[:, c[6]:c[6] + 96]
    z = w_in[:, c[6] + 96:]
    w = jnp.concatenate([q, z, ks, kw, kc, vc, vs, vw, gates,
                         jnp.zeros((D_MODEL, PROJ_TN - 96), F32)], axis=1).astype(BF16)
    proj = _norm_proj(x, norm_g, w[None], 0, tabs, query_tiles=(0, 1, 2, 3), rope_full_tiles=(8,))

    chunks = seq // CMP_STRIDE

    def chunked(col):
        t = proj[:, col:col + 256].reshape(batch, chunks, CMP_STRIDE, N_GROUPS, HEAD_DIM)
        return t.transpose(0, 3, 1, 2, 4).reshape(batch, N_GROUPS, chunks, CMP_STRIDE * HEAD_DIM)

    def w2_pair(w2, transposed):
        zero = jnp.zeros_like(w2)
        pair = jnp.stack([jnp.concatenate([w2, zero], axis=1), jnp.concatenate([zero, w2], axis=1)])
        return (pair.transpose(0, 2, 1) if transposed else pair).astype(BF16)

    half = CMP_STRIDE * HEAD_DIM
    kcmp = _compress(chunked(4608), kc_pos.reshape(2, half), kc_w1.astype(BF16),
                     w2_pair(kc_w2, False), False)
    vcmp_t = _compress(chunked(4864), vc_pos.reshape(2, half), vc_w1.astype(BF16),
                       w2_pair(vc_w2, True), True)

    ovt, n_cmp = _overlap_t(seq)
    gt = proj[:, 5632:5632 + 3 * N_HEADS].reshape(batch, seq, 3, 2, 2 * GROUP).transpose(0, 2, 3, 4, 1)
    o_cmp, mbt = _cmp_select(proj, kcmp, vcmp_t, ovt, gt, batch, seq, n_cmp)
    e_all = jnp.asarray((np.arange(seq)[:, None] // SLC_BLOCK) == np.arange(LANES)[None, :], BF16)
    kb = 256
    o_slc = _flash(proj, _values_t(proj, 5120, batch, seq, kb), batch, seq, k_col=32,
                   window=seq, kb=kb, mbt=mbt, e_all=e_all, gate=(gt, 1))
    o_win = _flash(proj, _values_t(proj, 5376, batch, seq, kb), batch, seq, k_col=34,
                   window=B_WINDOW, kb=kb, gate=(gt, 2))

    return _out_proj(x, w_out.astype(BF16), "b", (o_cmp, o_slc, o_win, proj, 2048), seq, final_w)


def _mixer_c(x, norm_g, w_in, layer, conv_w, w_out, tabs, seq, final_w=None):
    proj = _norm_proj(x, norm_g, w_in, layer, tabs)
    cw = jnp.concatenate([conv_w, jnp.zeros((8 - conv_w.shape[0], conv_w.shape[1]), F32)], axis=0)
    return _out_proj(x, w_out.astype(BF16), "c", (proj, cw), seq, final_w)


def kernel(x, positions, norm_w, final_norm_w, a_w_in, a_sinks, a_w_out, b_w_in, b_cmp_k_pos, b_cmp_k_w1,
           b_cmp_k_w2, b_cmp_v_pos, b_cmp_v_w1, b_cmp_v_w2, b_w_out, c_w_in, c_conv_w, c_w_out):
    batch, seq, d = x.shape
    depth = norm_w.shape[0]
    tabs = _rope_tables(positions)
    xf = x.reshape(batch * seq, d)
    for i in range(depth):
        kind, j = i % 3, i // 3
        fw = final_norm_w if i == depth - 1 else None
        if kind == 0:
            xf = _mixer_a(xf, norm_w[i], a_w_in, j, a_sinks[j], a_w_out[j], tabs, batch, seq, fw)
        elif kind == 1:
            xf = _mixer_b(xf, norm_w[i], b_w_in[j], b_cmp_k_pos[j], b_cmp_k_w1[j], b_cmp_k_w2[j],
                          b_cmp_v_pos[j], b_cmp_v_w1[j], b_cmp_v_w2[j], b_w_out[j], tabs, batch, seq, fw)
        else:
            xf = _mixer_c(xf, norm_w[i], c_w_in, j, c_conv_w[j], c_w_out[j], tabs, seq, fw)
    return xf.reshape(batch, seq, d)
```

```python
import functools
import math

import numpy as np
import jax
import jax.numpy as jnp
from jax import lax
from jax.experimental import pallas as pl
from jax.experimental.pallas import tpu as pltpu

F32 = jnp.float32
BF16 = jnp.bfloat16

D_MODEL = 2048
HEAD_DIM = 64
N_HEADS = 32
N_GROUPS = 4
GROUP = 8
ROPE_DIM = 16
ROPE_HALF = 8
ROPE_THETA = 500000.0
NORM_EPS = 1e-5
A_WINDOW = 128
B_WINDOW = 512
CMP_BLOCK = 32
CMP_STRIDE = 16
SLC_BLOCK = 64
SLC_TOPK = 16
CMP_HIDDEN = 256

LANES = 128
BF16_ROWS = 16
F32_ROWS = 8
NEG = -2.0 ** 100
LOG2E = math.log2(math.e)
_Q_SCALE = HEAD_DIM ** -0.5 * LOG2E
VMEM_LIMIT = 56 * 1024 * 1024

PROJ_TM = 1024
PROJ_TN = 512
OUT_TM = 256
ATT_QB = 128
SWA_SUB = 4
ACC_ROWS = HEAD_DIM + 16


def _params(sem):
    return pltpu.CompilerParams(dimension_semantics=sem, vmem_limit_bytes=VMEM_LIMIT)


def _norm_proj_kernel(x_ref, g_ref, w_ref, cos_ref, sa_ref, sb_ref, o_ref, h_ref, *,
                      query_tiles, rope_full_tiles, rope_half_tiles):
    j = pl.program_id(1)

    @pl.when(j == 0)
    def _():
        x = x_ref[...]
        ms = jnp.mean(x * x, axis=-1, keepdims=True)
        h_ref[...] = (x * lax.rsqrt(ms + NORM_EPS) * g_ref[...]).astype(BF16)

    acc = jnp.dot(h_ref[...], w_ref[0].astype(BF16), preferred_element_type=F32)
    tn = acc.shape[1]

    def rope(a):
        n = a.shape[1]
        reps = n // LANES
        cos = jnp.concatenate([cos_ref[...]] * reps, axis=1)
        sa = jnp.concatenate([sa_ref[...]] * reps, axis=1)
        sb = jnp.concatenate([sb_ref[...]] * reps, axis=1)
        return a * cos + pltpu.roll(a, n - ROPE_HALF, 1) * sa + pltpu.roll(a, ROPE_HALF, 1) * sb

    def any_of(tiles):
        c = j == tiles[0]
        for t in tiles[1:]:
            c = c | (j == t)
        return c

    plain = None
    if query_tiles:
        query = any_of(query_tiles)
        plain = ~query

        @pl.when(query)
        def _():
            o_ref[...] = (rope(acc) * _Q_SCALE).astype(o_ref.dtype)

    if rope_full_tiles:
        full = any_of(rope_full_tiles)
        plain = ~full if plain is None else plain & ~full

        @pl.when(full)
        def _():
            o_ref[...] = rope(acc).astype(o_ref.dtype)

    if rope_half_tiles:
        half = any_of(rope_half_tiles)
        plain = ~half if plain is None else plain & ~half

        @pl.when(half)
        def _():
            hw = tn // 2
            o_ref[:, :hw] = rope(acc[:, :hw]).astype(o_ref.dtype)
            o_ref[:, hw:] = acc[:, hw:].astype(o_ref.dtype)

    if plain is None:
        o_ref[...] = acc.astype(o_ref.dtype)
    else:
        @pl.when(plain)
        def _():
            o_ref[...] = acc.astype(o_ref.dtype)


def _norm_proj(x, g, w, layer, tabs, query_tiles=(), rope_full_tiles=(), rope_half_tiles=()):
    T, D = x.shape
    N = w.shape[2]
    tm, tn = min(PROJ_TM, T), PROJ_TN
    assert T % tm == 0 and N % tn == 0
    kern = functools.partial(_norm_proj_kernel, query_tiles=tuple(query_tiles),
                             rope_full_tiles=tuple(rope_full_tiles),
                             rope_half_tiles=tuple(rope_half_tiles))
    tab_spec = pl.BlockSpec((tm, LANES), lambda i, j: (i, 0))
    return pl.pallas_call(
        kern,
        grid=(T // tm, N // tn),
        in_specs=[pl.BlockSpec((tm, D), lambda i, j: (i, 0)),
                  pl.BlockSpec((1, D), lambda i, j: (0, 0)),
                  pl.BlockSpec((1, D, tn), lambda i, j: (layer, 0, j)),
                  tab_spec, tab_spec, tab_spec],
        out_specs=pl.BlockSpec((tm, tn), lambda i, j: (i, j)),
        out_shape=jax.ShapeDtypeStruct((T, N), BF16),
        scratch_shapes=[pltpu.VMEM((tm, D), BF16)],
        compiler_params=_params(("parallel", "arbitrary")),
        name="norm_proj",
    )(x, g.reshape(1, D), w, *tabs)


def _identity(n):
    return (lax.broadcasted_iota(jnp.int32, (n, n), 0)
            == lax.broadcasted_iota(jnp.int32, (n, n), 1)).astype(BF16)


def _queries_t(q_ref, rows=slice(None)):
    eye = _identity(LANES)
    return [lax.dot_general(eye, q_ref[rows, e * LANES:(e + 1) * LANES], (((1,), (1,)), ((), ())),
                            preferred_element_type=F32).astype(BF16) for e in range(GROUP)]


def _placed_queries(qts, hp, par):
    t = qts[par * (GROUP // 2) + hp]
    zero = jnp.zeros((HEAD_DIM, t.shape[1]), t.dtype)
    heads = (t[:HEAD_DIM], t[HEAD_DIM:])
    return [jnp.concatenate([zero, h] if par else [h, zero], axis=0) for h in heads]


def _gate_row(gt_ref, par, hp):
    g = 1.0 / (1.0 + jnp.exp(-gt_ref[0, 0, 0].astype(F32)))
    r = par * GROUP + 2 * hp
    return jnp.concatenate([g[r:r + 1], g[r + 1:r + 2]], axis=1)


def _tile_loop(lo, hi, fn, unroll):
    n_blocks = (hi - lo) // unroll

    def block(j, carry):
        fn([lo + j * unroll + u for u in range(unroll)])
        return carry

    def single(kt, carry):
        fn([kt])
        return carry

    lax.fori_loop(0, n_blocks, block, 0)
    lax.fori_loop(lo + n_blocks * unroll, hi, single, 0)


def _store_token_major(o_ref, get_ot, qb, rows=slice(None)):
    eye = _identity(qb)
    ots = [get_ot(par, hp) for par in (0, 1) for hp in range(GROUP // 2)]
    ots = [jnp.concatenate([ot[:, :qb], ot[:, qb:]], axis=0).astype(BF16) for ot in ots]
    outs = [lax.dot_general(eye, ot, (((1,), (1,)), ((), ())), preferred_element_type=F32)
            for ot in ots]
    for tile, o in enumerate(outs):
        o_ref[rows, tile * LANES:(tile + 1) * LANES] = o.astype(o_ref.dtype)


def _flash_kernel(*refs, seq, qb, kb, window, use_sel, use_gate):
    it = iter(refs)
    q_ref, k_ref, vt_ref = next(it), next(it), next(it)
    mbt_ref = next(it) if use_sel else None
    e_ref = next(it) if use_sel else None
    gt_ref = next(it) if use_gate else None
    o_ref, qa_scr, mx_scr, acc_scr, st_scr = (next(it) for _ in range(5))

    i = pl.program_id(2)
    q0 = i * qb
    ch = 2 * qb
    n_chain = 2 * (GROUP // 2)
    half = (GROUP // 2) * ch
    pack = mx_scr.shape[0]

    qts = _queries_t(q_ref)
    for c in range(n_chain):
        par, hp = divmod(c, GROUP // 2)
        cols = _placed_queries(qts, hp, par)
        if use_sel:
            cols = [jnp.concatenate([t, mbt_ref[0, par]], axis=0) for t in cols]
        qa_scr[:, c * ch:(c + 1) * ch] = jnp.concatenate(cols, axis=1)
    mx_scr[...] = jnp.full(mx_scr.shape, NEG, BF16)

    q_minus_k = (lax.broadcasted_iota(jnp.int32, (kb, qb), 1)
                 - lax.broadcasted_iota(jnp.int32, (kb, qb), 0))
    ones_rows = jnp.where(lax.broadcasted_iota(jnp.int32, (ACC_ROWS - HEAD_DIM, kb), 0) == 0,
                          1.0, 0.0).astype(BF16)
    last = (q0 + qb - 1) // kb
    banded = window < seq
    unroll = 4

    def scores(kt, masked, par, slot):
        k0 = kt * kb
        kl = pl.multiple_of(jnp.maximum(kt, 0) * kb, kb)
        ka = k_ref[pl.ds(kl, kb), :]
        if use_sel:
            ka = jnp.concatenate([ka, e_ref[pl.ds(kl, kb), :]], axis=1)
        if masked:
            d = q_minus_k + (q0 - k0)
            vis = d >= 0
            if banded:
                vis = vis & (d < window) & (kt >= 0)
            bias = jnp.where(vis, 0.0, NEG)
            bias = jnp.concatenate([bias, bias], axis=1)
        for hp in range(GROUP // 2):
            c = par * (GROUP // 2) + hp
            sl = slice(c * ch, (c + 1) * ch)
            st = jnp.dot(ka, qa_scr[:, sl], preferred_element_type=F32)
            st = (st + bias if masked else st).astype(BF16)
            st_scr[slot, :, sl] = st
            mx_scr[:, sl] = jnp.maximum(mx_scr[:, sl],
                                        jnp.max(st.reshape(kb // pack, pack, ch), axis=0))

    def reference(par):
        m = jnp.max(mx_scr[:, par * half:(par + 1) * half], axis=0, keepdims=True).astype(F32)
        refs_ = []
        for hp in range(GROUP // 2):
            asl = slice(hp * ch, (hp + 1) * ch)
            mc = m[:, asl]
            acc_scr[par, :, asl] = jnp.zeros((ACC_ROWS, ch), F32)
            refs_.append(mc.astype(BF16))
        return refs_

    def values(kts, slots, par, m_cols):
        vts = [vt_ref[0, 0, jnp.maximum(kt, 0)] for kt in kts]
        for hp in range(GROUP // 2):
            c = par * (GROUP // 2) + hp
            p = jnp.concatenate([jnp.exp2(st_scr[slot, :, c * ch:(c + 1) * ch] - m_cols[hp])
                                 for slot in slots], axis=0)
            vta = jnp.concatenate(
                [jnp.concatenate([vt[par * HEAD_DIM:(par + 1) * HEAD_DIM], ones_rows], axis=0)
                 for vt in vts], axis=1)
            asl = slice(hp * ch, (hp + 1) * ch)
            acc_scr[par, :, asl] += jnp.dot(vta, p, preferred_element_type=F32)

    if banded:
        n_band = st_scr.shape[0]
        kts = [last - (n_band - 1) + j for j in range(n_band)]
        slots = list(range(n_band))
        for kt, slot in zip(kts, slots):
            for par in (0, 1):
                scores(kt, True, par, slot)
        for par in (0, 1):
            values(kts, slots, par, reference(par))
    else:
        def all_scores(kts, masked):
            for kt in kts:
                for par in (0, 1):
                    scores(kt, masked, par, kt)

        _tile_loop(0, last, lambda kts: all_scores(kts, False), unroll)
        all_scores([last], True)
        m0, m1 = reference(0), reference(1)
        _tile_loop(0, last + 1, lambda kts: (values(kts, kts, 0, m0), values(kts, kts, 1, m1)), unroll)

    def get_ot(par, hp):
        acc = acc_scr[par][:, hp * ch:(hp + 1) * ch]
        o = acc[:HEAD_DIM] * (1.0 / acc[HEAD_DIM:HEAD_DIM + 1])
        return o * _gate_row(gt_ref, par, hp) if use_gate else o

    _store_token_major(o_ref, get_ot, qb)


def _flash(proj, vt, batch, seq, k_col, window, kb, mbt=None, e_all=None, gate=None):
    T = proj.shape[0]
    qb = min(2 * ATT_QB, seq) if window < seq else ATT_QB
    nq = seq // qb
    ch = 2 * qb
    n_chain = GROUP
    n_stage = min(seq // kb, (window + qb - 2) // kb + 1)
    use_sel, use_gate = mbt is not None, gate is not None
    kern = functools.partial(_flash_kernel, seq=seq, qb=qb, kb=kb, window=window,
                             use_sel=use_sel, use_gate=use_gate)
    in_specs, args = [], []
    in_specs += [pl.BlockSpec((qb, GROUP * LANES), lambda b, p, i: (b * nq + i, p)),
                 pl.BlockSpec((seq, LANES), lambda b, p, i: (b, k_col + p)),
                 pl.BlockSpec((1, 1, seq // kb, LANES, kb), lambda b, p, i: (b, p, 0, 0, 0))]
    args += [proj, proj, vt]
    if use_sel:
        in_specs += [pl.BlockSpec((1, 2, LANES, qb), lambda b, p, i: (b, p, 0, i)),
                     pl.BlockSpec((seq, LANES), lambda b, p, i: (0, 0))]
        args += [mbt, e_all]
    if use_gate:
        gt, br = gate
        in_specs.append(pl.BlockSpec((1, 1, 1, 2 * GROUP, qb), lambda b, p, i: (b, br, p, 0, i)))
        args.append(gt)
    return pl.pallas_call(
        kern,
        grid=(batch, 2, nq),
        in_specs=in_specs,
        out_specs=pl.BlockSpec((qb, GROUP * LANES), lambda b, p, i: (b * nq + i, p)),
        out_shape=jax.ShapeDtypeStruct((T, D_MODEL), BF16),
        scratch_shapes=[pltpu.VMEM((2 * LANES if use_sel else LANES, n_chain * ch), BF16),
                        pltpu.VMEM((BF16_ROWS, n_chain * ch), BF16),
                        pltpu.VMEM((2, ACC_ROWS, n_chain * ch // 2), F32),
                        pltpu.VMEM((n_stage, kb, n_chain * ch), BF16)],
        compiler_params=_params(("parallel", "parallel", "arbitrary")),
        name="flash_sel" if use_sel else "flash_win",
    )(*args)


def _swa_kernel(sinks_ref, q_ref, kp_ref, kc_ref, vp_ref, vc_ref, o_ref, *, qb, n_sub, window):
    pp = pl.program_id(1)
    i = pl.program_id(2)
    ch = 2 * qb
    half = (GROUP // 2) * ch
    subs = range(n_sub)
    k_tiles = [kp_ref[...]] + [kc_ref[s * qb:(s + 1) * qb, :] for s in subs]
    v_tiles = [vp_ref[0, 0, 0]] + [vc_ref[0, 0, s] for s in subs]

    qas = []
    for s in subs:
        qts = _queries_t(q_ref, slice(s * qb, (s + 1) * qb))
        qas.append(jnp.concatenate([c for par in (0, 1) for hp in range(GROUP // 2)
                                    for c in _placed_queries(qts, hp, par)], axis=1))
    sts = [jnp.dot(jnp.concatenate([k_tiles[s], k_tiles[s + 1]], axis=0), qas[s],
                   preferred_element_type=F32) for s in subs]

    row = lax.broadcasted_iota(jnp.int32, (2 * qb, qb), 0)
    d = lax.broadcasted_iota(jnp.int32, (2 * qb, qb), 1) + qb - row
    band = (d >= 0) & (d < window)
    sts = [(st + jnp.concatenate(
        [jnp.where(band & ((row >= qb) | (i > 0)) if s == 0 else band, 0.0, NEG)] * (2 * GROUP),
        axis=1)).astype(BF16) for s, st in zip(subs, sts)]

    sk = jnp.concatenate([jnp.full((1, qb), sinks_ref[(2 * pp + par) * GROUP + h], F32)
                          for par in (0, 1) for h in range(GROUP)], axis=1) * LOG2E
    ms = [jnp.maximum(jnp.max(st, axis=0, keepdims=True).astype(F32), sk).astype(BF16) for st in sts]
    ps = [jnp.exp2(st - m) for st, m in zip(sts, ms)]
    sink_ws = [jnp.exp2(sk - m.astype(F32)) for m in ms]

    ones_rows = jnp.where(lax.broadcasted_iota(jnp.int32, (ACC_ROWS - HEAD_DIM, 2 * qb), 0) == 0,
                          1.0, 0.0).astype(BF16)
    vts = [jnp.concatenate([v_tiles[s], v_tiles[s + 1]], axis=1) for s in subs]
    pvs = [[jnp.dot(jnp.concatenate([vts[s][par * HEAD_DIM:(par + 1) * HEAD_DIM], ones_rows], axis=0),
                    ps[s][:, par * half:(par + 1) * half], preferred_element_type=F32)
            for par in (0, 1)] for s in subs]
    for s in subs:
        outs = [pvs[s][par][:HEAD_DIM]
                * (1.0 / (pvs[s][par][HEAD_DIM:HEAD_DIM + 1] + sink_ws[s][:, par * half:(par + 1) * half]))
                for par in (0, 1)]
        _store_token_major(o_ref, lambda par, hp, outs=outs: outs[par][:, hp * ch:(hp + 1) * ch], qb,
                           slice(s * qb, (s + 1) * qb))


def _swa(proj, vt, sinks, batch, seq, k_col, window):
    T = proj.shape[0]
    qb, n_sub = ATT_QB, SWA_SUB
    assert window <= qb and seq % (n_sub * qb) == 0
    nq = seq // (n_sub * qb)
    prev = lambda i: jnp.maximum(n_sub * i - 1, 0)
    return pl.pallas_call(
        functools.partial(_swa_kernel, qb=qb, n_sub=n_sub, window=window),
        grid=(batch, 2, nq),
        in_specs=[pl.BlockSpec(memory_space=pltpu.SMEM),
                  pl.BlockSpec((n_sub * qb, GROUP * LANES), lambda b, p, i: (b * nq + i, p)),
                  pl.BlockSpec((qb, LANES), lambda b, p, i: (b * nq * n_sub + prev(i), k_col + p)),
                  pl.BlockSpec((n_sub * qb, LANES), lambda b, p, i: (b * nq + i, k_col + p)),
                  pl.BlockSpec((1, 1, 1, LANES, qb), lambda b, p, i: (b, p, prev(i), 0, 0)),
                  pl.BlockSpec((1, 1, n_sub, LANES, qb), lambda b, p, i: (b, p, i, 0, 0))],
        out_specs=pl.BlockSpec((n_sub * qb, GROUP * LANES), lambda b, p, i: (b * nq + i, p)),
        out_shape=jax.ShapeDtypeStruct((T, D_MODEL), BF16),
        compiler_params=_params(("parallel", "parallel", "arbitrary")),
        name="swa_sink",
    )(sinks, proj, proj, proj, vt, vt)


def _gelu_tanh(x):
    return 0.5 * x * (1.0 + jnp.tanh(math.sqrt(2.0 / math.pi) * (x + 0.044715 * (x * x * x))))


def _compress_kernel(u_ref, pos_ref, w1_ref, w2_ref, o_ref, *, transposed):
    out = None
    half = w1_ref.shape[0] // 2
    for par in (0, 1):
        u = u_ref[0, par].astype(F32)
        lo = (u + pos_ref[0:1, :]).astype(BF16)
        hi = (u + pos_ref[1:2, :]).astype(BF16)
        a = jnp.dot(lo, w1_ref[:half, :], preferred_element_type=F32)
        b = jnp.dot(hi, w1_ref[half:, :], preferred_element_type=F32)
        b_next = jnp.concatenate([b[1:], jnp.zeros((1, b.shape[1]), F32)], axis=0)
        hid = _gelu_tanh(a + b_next).astype(BF16)
        if transposed:
            term = lax.dot_general(w2_ref[par], hid, (((1,), (1,)), ((), ())),
                                   preferred_element_type=F32)
        else:
            term = jnp.dot(hid, w2_ref[par], preferred_element_type=F32)
        out = term if out is None else out + term
    o_ref[0, 0] = out.astype(o_ref.dtype)


def _compress(u, pos, w1, w2_pair, transposed):
    batch, _, chunks, feat = u.shape
    oshape = (LANES, chunks) if transposed else (chunks, LANES)
    return pl.pallas_call(
        functools.partial(_compress_kernel, transposed=transposed),
        grid=(batch, 2),
        in_specs=[pl.BlockSpec((1, 2, chunks, feat), lambda b, p: (b, p, 0, 0)),
                  pl.BlockSpec((2, feat), lambda b, p: (0, 0)),
                  pl.BlockSpec((2 * feat, CMP_HIDDEN), lambda b, p: (0, 0)),
                  pl.BlockSpec(w2_pair.shape, lambda b, p: (0, 0, 0))],
        out_specs=pl.BlockSpec((1, 1) + oshape, lambda b, p: (b, p, 0, 0)),
        out_shape=jax.ShapeDtypeStruct((batch, 2) + oshape, BF16),
        compiler_params=_params(("parallel", "parallel")),
        name="compress",
    )(u, pos, w1, w2_pair)


def _cmp_select_kernel(q_ref, kc_ref, vct_ref, ovt_ref, gt_ref, o_ref, mbt_ref, ot_scr, st_scr, *,
                       qb, n_cmp, n_sel, n_top):
    i = pl.program_id(2)
    q0 = i * qb
    nck = kc_ref.shape[2]
    ch = 2 * qb
    c_sub = lax.broadcasted_iota(jnp.int32, (nck, qb), 0)
    t_lane = q0 + lax.broadcasted_iota(jnp.int32, (nck, qb), 1)
    valid = (c_sub * CMP_STRIDE + (CMP_BLOCK - 1) <= t_lane) & (c_sub < n_cmp)
    blk = lax.broadcasted_iota(jnp.int32, (n_sel, qb), 0)
    cur = (q0 + lax.broadcasted_iota(jnp.int32, (n_sel, qb), 1)) // SLC_BLOCK
    kc = kc_ref[0, 0]
    qts = _queries_t(q_ref)

    for par in (0, 1):
        vt = vct_ref[0, 0][par * HEAD_DIM:(par + 1) * HEAD_DIM]
        heads = range(GROUP)
        qa = jnp.concatenate([c for hp in range(GROUP // 2)
                              for c in _placed_queries(qts, hp, par)], axis=1)
        st_scr[...] = jnp.dot(kc, qa, preferred_element_type=F32)
        ss = [jnp.where(valid, st_scr[:, h * qb:(h + 1) * qb], NEG) for h in heads]
        ms = [jnp.max(s, axis=0, keepdims=True) for s in ss]
        ms = [jnp.where(m > NEG, m, 0.0) for m in ms]
        exs = [jnp.exp2(s - m) for s, m in zip(ss, ms)]
        invs = [1.0 / jnp.maximum(jnp.sum(ex, axis=0, keepdims=True), 1e-30) for ex in exs]
        ps = [ex * inv for ex, inv in zip(exs, invs)]
        p_sum = functools.reduce(lambda a, b: a + b, ps)
        p_all = jnp.concatenate([p.astype(BF16) for p in ps], axis=1)
        o_all = jnp.dot(vt, p_all, preferred_element_type=F32)
        for hp in range(GROUP // 2):
            ot_scr[par, hp] = o_all[:, hp * ch:(hp + 1) * ch]

        imp_t = jnp.zeros((n_sel, qb), F32)
        rest = p_sum
        for _ in range(3):
            piece = rest.astype(BF16)
            rest = rest - piece.astype(F32)
            imp_t = imp_t + jnp.dot(ovt_ref[...], piece, preferred_element_type=F32)
        forced = (blk == 0) | (blk == cur) | (blk == cur - 1)
        imp_t = jnp.where(forced, jnp.inf, imp_t)
        imp_t = jnp.where(blk > cur, -jnp.inf, imp_t)
        groups = [imp_t[g * F32_ROWS:(g + 1) * F32_ROWS] for g in range(n_sel // F32_ROWS)]
        ranks = [jnp.zeros((F32_ROWS, qb), jnp.int32) for _ in groups]
        blk_in_group = lax.broadcasted_iota(jnp.int32, (F32_ROWS, qb), 0)
        for r in range(n_sel):
            row = imp_t[r:r + 1, :]
            for g, x in enumerate(groups):
                if g > r // F32_ROWS:
                    before = row >= x
                elif g < r // F32_ROWS:
                    before = row > x
                else:
                    before = (row > x) | ((row == x) & (blk_in_group > r % F32_ROWS))
                ranks[g] = ranks[g] + before.astype(jnp.int32)
        rank = jnp.concatenate(ranks, axis=0)
        bias_t = jnp.where(rank < n_top, 0.0, NEG)
        mbt_ref[0, par] = jnp.concatenate(
            [bias_t, jnp.zeros((LANES - n_sel, qb), F32)], axis=0).astype(mbt_ref.dtype)

    _store_token_major(o_ref, lambda par, hp: ot_scr[par, hp] * _gate_row(gt_ref, par, hp), qb)


def _cmp_select(proj, kcmp, vcmp_t, ovt, gt, batch, seq, n_cmp):
    qb = ATT_QB
    nq = seq // qb
    nck = kcmp.shape[2]
    n_sel = seq // SLC_BLOCK
    assert n_sel <= LANES
    kern = functools.partial(_cmp_select_kernel, qb=qb, n_cmp=n_cmp, n_sel=n_sel,
                             n_top=min(SLC_TOPK, n_sel))
    return pl.pallas_call(
        kern,
        grid=(batch, 2, nq),
        in_specs=[pl.BlockSpec((qb, GROUP * LANES), lambda b, p, i: (b * nq + i, p)),
                  pl.BlockSpec((1, 1, nck, LANES), lambda b, p, i: (b, p, 0, 0)),
                  pl.BlockSpec((1, 1, LANES, nck), lambda b, p, i: (b, p, 0, 0)),
                  pl.BlockSpec((n_sel, nck), lambda b, p, i: (0, 0)),
                  pl.BlockSpec((1, 1, 1, 2 * GROUP, qb), lambda b, p, i: (b, 0, p, 0, i))],
        out_specs=[pl.BlockSpec((qb, GROUP * LANES), lambda b, p, i: (b * nq + i, p)),
                   pl.BlockSpec((1, 2, LANES, qb), lambda b, p, i: (b, p, 0, i))],
        out_shape=[jax.ShapeDtypeStruct((batch * seq, D_MODEL), BF16),
                   jax.ShapeDtypeStruct((batch, N_GROUPS, LANES, seq), BF16)],
        scratch_shapes=[pltpu.VMEM((2, GROUP // 2, HEAD_DIM, 2 * qb), F32),
                        pltpu.VMEM((nck, GROUP * qb), F32)],
        compiler_params=_params(("parallel", "parallel", "arbitrary")),
        name="cmp_select",
    )(proj, kcmp, vcmp_t, ovt, gt)


def _silu(z):
    return z / (1.0 + jnp.exp(-z))


def _out_proj_kernel(*refs, mode, tm, seq, n_z, final_norm):
    it = iter(refs)
    x_ref = next(it)
    if mode == "a":
        o_ref = next(it)
        z = jnp.concatenate([next(it)[...] for _ in range(n_z)], axis=1)
        a = o_ref[...].astype(F32) * _silu(z.astype(F32))
    elif mode == "b":
        oc_ref, os_ref, ow_ref = (next(it) for _ in range(3))
        z = jnp.concatenate([next(it)[...] for _ in range(n_z)], axis=1)
        o = oc_ref[...].astype(F32) + os_ref[...].astype(F32) + ow_ref[...].astype(F32)
        a = o * _silu(z.astype(F32))
    else:
        u_ref, bg_ref, c_ref, z_ref, uh_ref, ch_ref, cw_ref = (next(it) for _ in range(7))
        v = c_ref[...].astype(F32) * u_ref[...].astype(F32)
        first = (pl.program_id(0) * tm) % seq == 0
        vh = ch_ref[...].astype(F32) * uh_ref[...].astype(F32)
        vh = jnp.where(first, 0.0, vh)
        h1, h2 = vh[7:8, :], vh[6:7, :]
        row = lax.broadcasted_iota(jnp.int32, v.shape, 0)
        v1 = jnp.where(row == 0, h1, pltpu.roll(v, 1, 0))
        v2 = jnp.where(row == 0, h2, jnp.where(row == 1, h1, pltpu.roll(v, 2, 0)))
        y = cw_ref[0:1, :] * v2 + cw_ref[1:2, :] * v1 + cw_ref[2:3, :] * v
        a = bg_ref[...].astype(F32) * y * _silu(z_ref[...].astype(F32))
    w_ref = next(it)
    fw_ref = next(it) if final_norm else None
    out_ref = next(it)
    xn = x_ref[...] + jnp.dot(a.astype(BF16), w_ref[...], preferred_element_type=F32)
    if final_norm:
        ms = jnp.mean(xn * xn, axis=-1, keepdims=True)
        xn = xn * lax.rsqrt(ms + NORM_EPS) * fw_ref[...]
    out_ref[...] = xn


def _out_proj(x, w_out, mode, inputs, seq, final_w=None):
    T, D = x.shape
    tm = OUT_TM
    assert T % tm == 0 and seq % tm == 0
    row = lambda c: pl.BlockSpec((tm, D), lambda i, c=c: (i, c))
    in_specs, args = [row(0)], [x]
    n_z = D // PROJ_TN
    if mode in ("a", "b"):
        *outs, proj, z_col = inputs
        assert z_col % PROJ_TN == 0
        in_specs += [row(0)] * len(outs)
        in_specs += [pl.BlockSpec((tm, PROJ_TN), lambda i, c=z_col // PROJ_TN + n: (i, c))
                     for n in range(n_z)]
        args += outs + [proj] * n_z
    else:
        proj, conv_w = inputs
        halo = lambda c: pl.BlockSpec((8, D), lambda i, c=c: (jnp.maximum(i * (tm // 8) - 1, 0), c))
        in_specs += [row(0), row(1), row(2), row(3), halo(0), halo(2),
                     pl.BlockSpec(conv_w.shape, lambda i: (0, 0))]
        args += [proj, proj, proj, proj, proj, proj, conv_w]
    in_specs.append(pl.BlockSpec((D, D), lambda i: (0, 0)))
    args.append(w_out)
    if final_w is not None:
        in_specs.append(pl.BlockSpec((1, D), lambda i: (0, 0)))
        args.append(final_w.reshape(1, D))
    kern = functools.partial(_out_proj_kernel, mode=mode, tm=tm, seq=seq, n_z=n_z,
                             final_norm=final_w is not None)
    return pl.pallas_call(
        kern,
        grid=(T // tm,),
        in_specs=in_specs,
        out_specs=pl.BlockSpec((tm, D), lambda i: (i, 0)),
        out_shape=jax.ShapeDtypeStruct((T, D), F32),
        compiler_params=_params(("parallel",)),
        name="out_proj_" + mode,
    )(*args)


def _rope_tables(positions):
    inv_freq = ROPE_THETA ** (-jnp.arange(0, ROPE_DIM, 2, dtype=F32) / ROPE_DIM)
    head = jnp.concatenate([inv_freq, inv_freq, jnp.zeros((HEAD_DIM - ROPE_DIM,), F32)])
    ang = positions.astype(F32).reshape(-1, 1) * jnp.concatenate([head, head])[None, :]
    cos, sin = jnp.cos(ang), jnp.sin(ang)
    d = np.arange(LANES) % HEAD_DIM
    sin_a = jnp.where(jnp.asarray(d < ROPE_HALF), -sin, 0.0)
    sin_b = jnp.where(jnp.asarray((d >= ROPE_HALF) & (d < ROPE_DIM)), sin, 0.0)
    return cos, sin_a, sin_b


def _overlap_t(seq):
    nc = (seq - CMP_BLOCK) // CMP_STRIDE + 1
    nsel = seq // SLC_BLOCK
    c_start = np.arange(nc) * CMP_STRIDE
    c_end = c_start + CMP_BLOCK
    s_start = np.arange(nsel) * SLC_BLOCK
    s_end = s_start + SLC_BLOCK
    ov = np.clip(np.minimum(c_end[:, None], s_end[None, :]) - np.maximum(c_start[:, None], s_start[None, :]), 0, None)
    ov = (ov / CMP_BLOCK).astype(np.float32)
    ovt = np.zeros((nsel, seq // CMP_STRIDE), np.float32)
    ovt[:, :nc] = ov.T
    return jnp.asarray(ovt, BF16), nc


def _values_t(proj, col, batch, seq, kb):
    v = proj[:, col:col + 2 * LANES].reshape(batch, seq // kb, kb, 2, LANES)
    return v.transpose(0, 3, 1, 4, 2)


def _mixer_a(x, norm_g, w_in, layer, sinks, w_out, tabs, batch, seq, final_w=None):
    proj = _norm_proj(x, norm_g, w_in, layer, tabs, query_tiles=(0, 1, 2, 3), rope_half_tiles=(4,))
    o = _swa(proj, _values_t(proj, 2304, batch, seq, ATT_QB), sinks, batch, seq, k_col=16,
             window=A_WINDOW)
    return _out_proj(x, w_out.astype(BF16), "a", (o, proj, 2560), seq, final_w)


def _mixer_b(x, norm_g, w_in, kc_pos, kc_w1, kc_w2, vc_pos, vc_w1, vc_w2, w_out, tabs, batch, seq,
             final_w=None):
    c = [2048 + 256 * n for n in range(7)]
    q = w_in[:, :2048]
    kc, vc, ks, vs, kw, vw = (w_in[:, c[n]:c[n + 1]] for n in range(6))
    gates = w_in[:, c[6]:c[6] + 96]
    z = w_in[:, c[6] + 96:]
    w = jnp.concatenate([q, z, ks, kw, kc, vc, vs, vw, gates,
                         jnp.zeros((D_MODEL, PROJ_TN - 96), F32)], axis=1).astype(BF16)
    proj = _norm_proj(x, norm_g, w[None], 0, tabs, query_tiles=(0, 1, 2, 3), rope_full_tiles=(8,))

    chunks = seq // CMP_STRIDE

    def chunked(col):
        t = proj[:, col:col + 256].reshape(batch, chunks, CMP_STRIDE, N_GROUPS, HEAD_DIM)
        return t.transpose(0, 3, 1, 2, 4).reshape(batch, N_GROUPS, chunks, CMP_STRIDE * HEAD_DIM)

    def w2_pair(w2, transposed):
        zero = jnp.zeros_like(w2)
        pair = jnp.stack([jnp.concatenate([w2, zero], axis=1), jnp.concatenate([zero, w2], axis=1)])
        return (pair.transpose(0, 2, 1) if transposed else pair).astype(BF16)

    half = CMP_STRIDE * HEAD_DIM
    kcmp = _compress(chunked(4608), kc_pos.reshape(2, half), kc_w1.astype(BF16),
                     w2_pair(kc_w2, False), False)
    vcmp_t = _compress(chunked(4864), vc_pos.reshape(2, half), vc_w1.astype(BF16),
                       w2_pair(vc_w2, True), True)

    ovt, n_cmp = _overlap_t(seq)
    gt = proj[:, 5632:5632 + 3 * N_HEADS].reshape(batch, seq, 3, 2, 2 * GROUP).transpose(0, 2, 3, 4, 1)
    o_cmp, mbt = _cmp_select(proj, kcmp, vcmp_t, ovt, gt, batch, seq, n_cmp)
    e_all = jnp.asarray((np.arange(seq)[:, None] // SLC_BLOCK) == np.arange(LANES)[None, :], BF16)
    kb = 256
    o_slc = _flash(proj, _values_t(proj, 5120, batch, seq, kb), batch, seq, k_col=32,
                   window=seq, kb=kb, mbt=mbt, e_all=e_all, gate=(gt, 1))
    o_win = _flash(proj, _values_t(proj, 5376, batch, seq, kb), batch, seq, k_col=34,
                   window=B_WINDOW, kb=kb, gate=(gt, 2))

    return _out_proj(x, w_out.astype(BF16), "b", (o_cmp, o_slc, o_win, proj, 2048), seq, final_w)


def _mixer_c(x, norm_g, w_in, layer, conv_w, w_out, tabs, seq, final_w=None):
    proj = _norm_proj(x, norm_g, w_in, layer, tabs)
    cw = jnp.concatenate([conv_w, jnp.zeros((8 - conv_w.shape[0], conv_w.shape[1]), F32)], axis=0)
    return _out_proj(x, w_out.astype(BF16), "c", (proj, cw), seq, final_w)


def kernel(x, positions, norm_w, final_norm_w, a_w_in, a_sinks, a_w_out, b_w_in, b_cmp_k_pos, b_cmp_k_w1,
           b_cmp_k_w2, b_cmp_v_pos, b_cmp_v_w1, b_cmp_v_w2, b_w_out, c_w_in, c_conv_w, c_w_out):
    batch, seq, d = x.shape
    depth = norm_w.shape[0]
    tabs = _rope_tables(positions)
    xf = x.reshape(batch * seq, d)
    for i in range(depth):
        kind, j = i % 3, i // 3
        fw = final_norm_w if i == depth - 1 else None
        if kind == 0:
            xf = _mixer_a(xf, norm_w[i], a_w_in, j, a_sinks[j], a_w_out[j], tabs, batch, seq, fw)
        elif kind == 1:
            xf = _mixer_b(xf, norm_w[i], b_w_in[j], b_cmp_k_pos[j], b_cmp_k_w1[j], b_cmp_k_w2[j],
                          b_cmp_v_pos[j], b_cmp_v_w1[j], b_cmp_v_w2[j], b_w_out[j], tabs, batch, seq, fw)
        else:
            xf = _mixer_c(xf, norm_w[i], c_w_in, j, c_conv_w[j], c_w_out[j], tabs, seq, fw)
    return xf.reshape(batch, seq, d)
```

```python
import functools
import math

import numpy as np
import jax
import jax.numpy as jnp
from jax import lax
from jax.experimental import pallas as pl
from jax.experimental.pallas import tpu as pltpu

F32 = jnp.float32
BF16 = jnp.bfloat16

D_MODEL = 2048
HEAD_DIM = 64
N_HEADS = 32
N_GROUPS = 4
GROUP = 8
ROPE_DIM = 16
ROPE_HALF = 8
ROPE_THETA = 500000.0
NORM_EPS = 1e-5
A_WINDOW = 128
B_WINDOW = 512
CMP_BLOCK = 32
CMP_STRIDE = 16
SLC_BLOCK = 64
SLC_TOPK = 16
CMP_HIDDEN = 256

LANES = 128
BF16_ROWS = 16
F32_ROWS = 8
NEG = -2.0 ** 100
LOG2E = math.log2(math.e)
_Q_SCALE = HEAD_DIM ** -0.5 * LOG2E
VMEM_LIMIT = 56 * 1024 * 1024

PROJ_TM = 1024
PROJ_TN = 512
OUT_TM = 256
ATT_QB = 128
SWA_SUB = 4
ACC_ROWS = HEAD_DIM + 16


def _params(sem):
    return pltpu.CompilerParams(dimension_semantics=sem, vmem_limit_bytes=VMEM_LIMIT)


def _norm_proj_kernel(x_ref, g_ref, w_ref, cos_ref, sa_ref, sb_ref, o_ref, h_ref, *,
                      query_tiles, rope_full_tiles, rope_half_tiles):
    j = pl.program_id(1)

    @pl.when(j == 0)
    def _():
        x = x_ref[...]
        ms = jnp.mean(x * x, axis=-1, keepdims=True)
        h_ref[...] = (x * lax.rsqrt(ms + NORM_EPS) * g_ref[...]).astype(BF16)

    acc = jnp.dot(h_ref[...], w_ref[0].astype(BF16), preferred_element_type=F32)
    tn = acc.shape[1]

    def rope(a):
        n = a.shape[1]
        reps = n // LANES
        cos = jnp.concatenate([cos_ref[...]] * reps, axis=1)
        sa = jnp.concatenate([sa_ref[...]] * reps, axis=1)
        sb = jnp.concatenate([sb_ref[...]] * reps, axis=1)
        return a * cos + pltpu.roll(a, n - ROPE_HALF, 1) * sa + pltpu.roll(a, ROPE_HALF, 1) * sb

    def any_of(tiles):
        c = j == tiles[0]
        for t in tiles[1:]:
            c = c | (j == t)
        return c

    plain = None
    if query_tiles:
        query = any_of(query_tiles)
        plain = ~query

        @pl.when(query)
        def _():
            o_ref[...] = (rope(acc) * _Q_SCALE).astype(o_ref.dtype)

    if rope_full_tiles:
        full = any_of(rope_full_tiles)
        plain = ~full if plain is None else plain & ~full

        @pl.when(full)
        def _():
            o_ref[...] = rope(acc).astype(o_ref.dtype)

    if rope_half_tiles:
        half = any_of(rope_half_tiles)
        plain = ~half if plain is None else plain & ~half

        @pl.when(half)
        def _():
            hw = tn // 2
            o_ref[:, :hw] = rope(acc[:, :hw]).astype(o_ref.dtype)
            o_ref[:, hw:] = acc[:, hw:].astype(o_ref.dtype)

    if plain is None:
        o_ref[...] = acc.astype(o_ref.dtype)
    else:
        @pl.when(plain)
        def _():
            o_ref[...] = acc.astype(o_ref.dtype)


def _norm_proj(x, g, w, layer, tabs, query_tiles=(), rope_full_tiles=(), rope_half_tiles=()):
    T, D = x.shape
    N = w.shape[2]
    tm, tn = min(PROJ_TM, T), PROJ_TN
    assert T % tm == 0 and N % tn == 0
    kern = functools.partial(_norm_proj_kernel, query_tiles=tuple(query_tiles),
                             rope_full_tiles=tuple(rope_full_tiles),
                             rope_half_tiles=tuple(rope_half_tiles))
    tab_spec = pl.BlockSpec((tm, LANES), lambda i, j: (i, 0))
    return pl.pallas_call(
        kern,
        grid=(T // tm, N // tn),
        in_specs=[pl.BlockSpec((tm, D), lambda i, j: (i, 0)),
                  pl.BlockSpec((1, D), lambda i, j: (0, 0)),
                  pl.BlockSpec((1, D, tn), lambda i, j: (layer, 0, j)),
                  tab_spec, tab_spec, tab_spec],
        out_specs=pl.BlockSpec((tm, tn), lambda i, j: (i, j)),
        out_shape=jax.ShapeDtypeStruct((T, N), BF16),
        scratch_shapes=[pltpu.VMEM((tm, D), BF16)],
        compiler_params=_params(("parallel", "arbitrary")),
        name="norm_proj",
    )(x, g.reshape(1, D), w, *tabs)


def _identity(n):
    return (lax.broadcasted_iota(jnp.int32, (n, n), 0)
            == lax.broadcasted_iota(jnp.int32, (n, n), 1)).astype(BF16)


def _queries_t(q_ref, rows=slice(None)):
    eye = _identity(LANES)
    return [lax.dot_general(eye, q_ref[rows, e * LANES:(e + 1) * LANES], (((1,), (1,)), ((), ())),
                            preferred_element_type=F32).astype(BF16) for e in range(GROUP)]


def _placed_queries(qts, hp, par):
    t = qts[par * (GROUP // 2) + hp]
    zero = jnp.zeros((HEAD_DIM, t.shape[1]), t.dtype)
    heads = (t[:HEAD_DIM], t[HEAD_DIM:])
    return [jnp.concatenate([zero, h] if par else [h, zero], axis=0) for h in heads]


def _gate_row(gt_ref, par, hp):
    g = 1.0 / (1.0 + jnp.exp(-gt_ref[0, 0, 0].astype(F32)))
    r = par * GROUP + 2 * hp
    return jnp.concatenate([g[r:r + 1], g[r + 1:r + 2]], axis=1)


def _tile_loop(lo, hi, fn, unroll):
    n_blocks = (hi - lo) // unroll

    def block(j, carry):
        fn([lo + j * unroll + u for u in range(unroll)])
        return carry

    def single(kt, carry):
        fn([kt])
        return carry

    lax.fori_loop(0, n_blocks, block, 0)
    lax.fori_loop(lo + n_blocks * unroll, hi, single, 0)


def _store_token_major(o_ref, get_ot, qb, rows=slice(None)):
    eye = _identity(qb)
    ots = [get_ot(par, hp) for par in (0, 1) for hp in range(GROUP // 2)]
    ots = [jnp.concatenate([ot[:, :qb], ot[:, qb:]], axis=0).astype(BF16) for ot in ots]
    outs = [lax.dot_general(eye, ot, (((1,), (1,)), ((), ())), preferred_element_type=F32)
            for ot in ots]
    for tile, o in enumerate(outs):
        o_ref[rows, tile * LANES:(tile + 1) * LANES] = o.astype(o_ref.dtype)


def _flash_kernel(*refs, seq, qb, kb, window, use_sel, use_gate):
    it = iter(refs)
    q_ref, k_ref, vt_ref = next(it), next(it), next(it)
    mbt_ref = next(it) if use_sel else None
    e_ref = next(it) if use_sel else None
    gt_ref = next(it) if use_gate else None
    o_ref, qa_scr, mx_scr, acc_scr, st_scr = (next(it) for _ in range(5))

    i = pl.program_id(2)
    q0 = i * qb
    ch = 2 * qb
    n_chain = 2 * (GROUP // 2)
    half = (GROUP // 2) * ch
    pack = mx_scr.shape[0]

    qts = _queries_t(q_ref)
    for c in range(n_chain):
        par, hp = divmod(c, GROUP // 2)
        cols = _placed_queries(qts, hp, par)
        if use_sel:
            cols = [jnp.concatenate([t, mbt_ref[0, par]], axis=0) for t in cols]
        qa_scr[:, c * ch:(c + 1) * ch] = jnp.concatenate(cols, axis=1)
    mx_scr[...] = jnp.full(mx_scr.shape, NEG, BF16)

    q_minus_k = (lax.broadcasted_iota(jnp.int32, (kb, qb), 1)
                 - lax.broadcasted_iota(jnp.int32, (kb, qb), 0))
    ones_rows = jnp.where(lax.broadcasted_iota(jnp.int32, (ACC_ROWS - HEAD_DIM, kb), 0) == 0,
                          1.0, 0.0).astype(BF16)
    last = (q0 + qb - 1) // kb
    banded = window < seq
    unroll = 4

    def scores(kt, masked, par, slot):
        k0 = kt * kb
        kl = pl.multiple_of(jnp.maximum(kt, 0) * kb, kb)
        ka = k_ref[pl.ds(kl, kb), :]
        if use_sel:
            ka = jnp.concatenate([ka, e_ref[pl.ds(kl, kb), :]], axis=1)
        if masked:
            d = q_minus_k + (q0 - k0)
            vis = d >= 0
            if banded:
                vis = vis & (d < window) & (kt >= 0)
            bias = jnp.where(vis, 0.0, NEG)
            bias = jnp.concatenate([bias, bias], axis=1)
        for hp in range(GROUP // 2):
            c = par * (GROUP // 2) + hp
            sl = slice(c * ch, (c + 1) * ch)
            st = jnp.dot(ka, qa_scr[:, sl], preferred_element_type=F32)
            st = (st + bias if masked else st).astype(BF16)
            st_scr[slot, :, sl] = st
            mx_scr[:, sl] = jnp.maximum(mx_scr[:, sl],
                                        jnp.max(st.reshape(kb // pack, pack, ch), axis=0))

    def reference(par):
        m = jnp.max(mx_scr[:, par * half:(par + 1) * half], axis=0, keepdims=True).astype(F32)
        refs_ = []
        for hp in range(GROUP // 2):
            asl = slice(hp * ch, (hp + 1) * ch)
            mc = m[:, asl]
            acc_scr[par, :, asl] = jnp.zeros((ACC_ROWS, ch), F32)
            refs_.append(mc.astype(BF16))
        return refs_

    def values(kts, slots, par, m_cols):
        vts = [vt_ref[0, 0, jnp.maximum(kt, 0)] for kt in kts]
        for hp in range(GROUP // 2):
            c = par * (GROUP // 2) + hp
            p = jnp.concatenate([jnp.exp2(st_scr[slot, :, c * ch:(c + 1) * ch] - m_cols[hp])
                                 for slot in slots], axis=0)
            vta = jnp.concatenate(
                [jnp.concatenate([vt[par * HEAD_DIM:(par + 1) * HEAD_DIM], ones_rows], axis=0)
                 for vt in vts], axis=1)
            asl = slice(hp * ch, (hp + 1) * ch)
            acc_scr[par, :, asl] += jnp.dot(vta, p, preferred_element_type=F32)

    if banded:
        n_band = st_scr.shape[0]
        kts = [last - (n_band - 1) + j for j in range(n_band)]
        slots = list(range(n_band))
        for kt, slot in zip(kts, slots):
            for par in (0, 1):
                scores(kt, True, par, slot)
        for par in (0, 1):
            values(kts, slots, par, reference(par))
    else:
        def all_scores(kts, masked):
            for kt in kts:
                for par in (0, 1):
                    scores(kt, masked, par, kt)

        _tile_loop(0, last, lambda kts: all_scores(kts, False), unroll)
        all_scores([last], True)
        m0, m1 = reference(0), reference(1)
        _tile_loop(0, last + 1, lambda kts: (values(kts, kts, 0, m0), values(kts, kts, 1, m1)), unroll)

    def get_ot(par, hp):
        acc = acc_scr[par][:, hp * ch:(hp + 1) * ch]
        o = acc[:HEAD_DIM] * (1.0 / acc[HEAD_DIM:HEAD_DIM + 1])
        return o * _gate_row(gt_ref, par, hp) if use_gate else o

    _store_token_major(o_ref, get_ot, qb)


def _flash(proj, vt, batch, seq, k_col, window, kb, mbt=None, e_all=None, gate=None):
    T = proj.shape[0]
    qb = min(2 * ATT_QB, seq)
    nq = seq // qb
    ch = 2 * qb
    n_chain = GROUP
    n_stage = min(seq // kb, (window + qb - 2) // kb + 1)
    use_sel, use_gate = mbt is not None, gate is not None
    kern = functools.partial(_flash_kernel, seq=seq, qb=qb, kb=kb, window=window,
                             use_sel=use_sel, use_gate=use_gate)
    in_specs, args = [], []
    in_specs += [pl.BlockSpec((qb, GROUP * LANES), lambda b, p, i: (b * nq + i, p)),
                 pl.BlockSpec((seq, LANES), lambda b, p, i: (b, k_col + p)),
                 pl.BlockSpec((1, 1, seq // kb, LANES, kb), lambda b, p, i: (b, p, 0, 0, 0))]
    args += [proj, proj, vt]
    if use_sel:
        in_specs += [pl.BlockSpec((1, 2, LANES, qb), lambda b, p, i: (b, p, 0, i)),
                     pl.BlockSpec((seq, LANES), lambda b, p, i: (0, 0))]
        args += [mbt, e_all]
    if use_gate:
        gt, br = gate
        in_specs.append(pl.BlockSpec((1, 1, 1, 2 * GROUP, qb), lambda b, p, i: (b, br, p, 0, i)))
        args.append(gt)
    return pl.pallas_call(
        kern,
        grid=(batch, 2, nq),
        in_specs=in_specs,
        out_specs=pl.BlockSpec((qb, GROUP * LANES), lambda b, p, i: (b * nq + i, p)),
        out_shape=jax.ShapeDtypeStruct((T, D_MODEL), BF16),
        scratch_shapes=[pltpu.VMEM((2 * LANES if use_sel else LANES, n_chain * ch), BF16),
                        pltpu.VMEM((BF16_ROWS, n_chain * ch), BF16),
                        pltpu.VMEM((2, ACC_ROWS, n_chain * ch // 2), F32),
                        pltpu.VMEM((n_stage, kb, n_chain * ch), BF16)],
        compiler_params=_params(("parallel", "parallel", "arbitrary")),
        name="flash_sel" if use_sel else "flash_win",
    )(*args)


def _swa_kernel(sinks_ref, q_ref, kp_ref, kc_ref, vp_ref, vc_ref, o_ref, *, qb, n_sub, window):
    pp = pl.program_id(1)
    i = pl.program_id(2)
    ch = 2 * qb
    half = (GROUP // 2) * ch
    subs = range(n_sub)
    k_tiles = [kp_ref[...]] + [kc_ref[s * qb:(s + 1) * qb, :] for s in subs]
    v_tiles = [vp_ref[0, 0, 0]] + [vc_ref[0, 0, s] for s in subs]

    qas = []
    for s in subs:
        qts = _queries_t(q_ref, slice(s * qb, (s + 1) * qb))
        qas.append(jnp.concatenate([c for par in (0, 1) for hp in range(GROUP // 2)
                                    for c in _placed_queries(qts, hp, par)], axis=1))
    sts = [jnp.dot(jnp.concatenate([k_tiles[s], k_tiles[s + 1]], axis=0), qas[s],
                   preferred_element_type=F32) for s in subs]

    row = lax.broadcasted_iota(jnp.int32, (2 * qb, qb), 0)
    d = lax.broadcasted_iota(jnp.int32, (2 * qb, qb), 1) + qb - row
    band = (d >= 0) & (d < window)
    sts = [(st + jnp.concatenate(
        [jnp.where(band & ((row >= qb) | (i > 0)) if s == 0 else band, 0.0, NEG)] * (2 * GROUP),
        axis=1)).astype(BF16) for s, st in zip(subs, sts)]

    sk = jnp.concatenate([jnp.full((1, qb), sinks_ref[(2 * pp + par) * GROUP + h], F32)
                          for par in (0, 1) for h in range(GROUP)], axis=1) * LOG2E
    ms = [jnp.maximum(jnp.max(st, axis=0, keepdims=True).astype(F32), sk).astype(BF16) for st in sts]
    ps = [jnp.exp2(st - m) for st, m in zip(sts, ms)]
    sink_ws = [jnp.exp2(sk - m.astype(F32)) for m in ms]

    ones_rows = jnp.where(lax.broadcasted_iota(jnp.int32, (ACC_ROWS - HEAD_DIM, 2 * qb), 0) == 0,
                          1.0, 0.0).astype(BF16)
    vts = [jnp.concatenate([v_tiles[s], v_tiles[s + 1]], axis=1) for s in subs]
    pvs = [[jnp.dot(jnp.concatenate([vts[s][par * HEAD_DIM:(par + 1) * HEAD_DIM], ones_rows], axis=0),
                    ps[s][:, par * half:(par + 1) * half], preferred_element_type=F32)
            for par in (0, 1)] for s in subs]
    for s in subs:
        outs = [pvs[s][par][:HEAD_DIM]
                * (1.0 / (pvs[s][par][HEAD_DIM:HEAD_DIM + 1] + sink_ws[s][:, par * half:(par + 1) * half]))
                for par in (0, 1)]
        _store_token_major(o_ref, lambda par, hp, outs=outs: outs[par][:, hp * ch:(hp + 1) * ch], qb,
                           slice(s * qb, (s + 1) * qb))


def _swa(proj, vt, sinks, batch, seq, k_col, window):
    T = proj.shape[0]
    qb, n_sub = ATT_QB, SWA_SUB
    assert window <= qb and seq % (n_sub * qb) == 0
    nq = seq // (n_sub * qb)
    prev = lambda i: jnp.maximum(n_sub * i - 1, 0)
    return pl.pallas_call(
        functools.partial(_swa_kernel, qb=qb, n_sub=n_sub, window=window),
        grid=(batch, 2, nq),
        in_specs=[pl.BlockSpec(memory_space=pltpu.SMEM),
                  pl.BlockSpec((n_sub * qb, GROUP * LANES), lambda b, p, i: (b * nq + i, p)),
                  pl.BlockSpec((qb, LANES), lambda b, p, i: (b * nq * n_sub + prev(i), k_col + p)),
                  pl.BlockSpec((n_sub * qb, LANES), lambda b, p, i: (b * nq + i, k_col + p)),
                  pl.BlockSpec((1, 1, 1, LANES, qb), lambda b, p, i: (b, p, prev(i), 0, 0)),
                  pl.BlockSpec((1, 1, n_sub, LANES, qb), lambda b, p, i: (b, p, i, 0, 0))],
        out_specs=pl.BlockSpec((n_sub * qb, GROUP * LANES), lambda b, p, i: (b * nq + i, p)),
        out_shape=jax.ShapeDtypeStruct((T, D_MODEL), BF16),
        compiler_params=_params(("parallel", "parallel", "arbitrary")),
        name="swa_sink",
    )(sinks, proj, proj, proj, vt, vt)


def _gelu_tanh(x):
    return 0.5 * x * (1.0 + jnp.tanh(math.sqrt(2.0 / math.pi) * (x + 0.044715 * (x * x * x))))


def _compress_kernel(u_ref, pos_ref, w1_ref, w2_ref, o_ref, *, transposed):
    out = None
    half = w1_ref.shape[0] // 2
    for par in (0, 1):
        u = u_ref[0, par].astype(F32)
        lo = (u + pos_ref[0:1, :]).astype(BF16)
        hi = (u + pos_ref[1:2, :]).astype(BF16)
        a = jnp.dot(lo, w1_ref[:half, :], preferred_element_type=F32)
        b = jnp.dot(hi, w1_ref[half:, :], preferred_element_type=F32)
        b_next = jnp.concatenate([b[1:], jnp.zeros((1, b.shape[1]), F32)], axis=0)
        hid = _gelu_tanh(a + b_next).astype(BF16)
        if transposed:
            term = lax.dot_general(w2_ref[par], hid, (((1,), (1,)), ((), ())),
                                   preferred_element_type=F32)
        else:
            term = jnp.dot(hid, w2_ref[par], preferred_element_type=F32)
        out = term if out is None else out + term
    o_ref[0, 0] = out.astype(o_ref.dtype)


def _compress(u, pos, w1, w2_pair, transposed):
    batch, _, chunks, feat = u.shape
    oshape = (LANES, chunks) if transposed else (chunks, LANES)
    return pl.pallas_call(
        functools.partial(_compress_kernel, transposed=transposed),
        grid=(batch, 2),
        in_specs=[pl.BlockSpec((1, 2, chunks, feat), lambda b, p: (b, p, 0, 0)),
                  pl.BlockSpec((2, feat), lambda b, p: (0, 0)),
                  pl.BlockSpec((2 * feat, CMP_HIDDEN), lambda b, p: (0, 0)),
                  pl.BlockSpec(w2_pair.shape, lambda b, p: (0, 0, 0))],
        out_specs=pl.BlockSpec((1, 1) + oshape, lambda b, p: (b, p, 0, 0)),
        out_shape=jax.ShapeDtypeStruct((batch, 2) + oshape, BF16),
        compiler_params=_params(("parallel", "parallel")),
        name="compress",
    )(u, pos, w1, w2_pair)


def _cmp_select_kernel(q_ref, kc_ref, vct_ref, ovt_ref, gt_ref, o_ref, mbt_ref, ot_scr, st_scr, *,
                       qb, n_cmp, n_sel, n_top):
    i = pl.program_id(2)
    q0 = i * qb
    nck = kc_ref.shape[2]
    ch = 2 * qb
    c_sub = lax.broadcasted_iota(jnp.int32, (nck, qb), 0)
    t_lane = q0 + lax.broadcasted_iota(jnp.int32, (nck, qb), 1)
    valid = (c_sub * CMP_STRIDE + (CMP_BLOCK - 1) <= t_lane) & (c_sub < n_cmp)
    blk = lax.broadcasted_iota(jnp.int32, (n_sel, qb), 0)
    cur = (q0 + lax.broadcasted_iota(jnp.int32, (n_sel, qb), 1)) // SLC_BLOCK
    kc = kc_ref[0, 0]
    qts = _queries_t(q_ref)

    for par in (0, 1):
        vt = vct_ref[0, 0][par * HEAD_DIM:(par + 1) * HEAD_DIM]
        heads = range(GROUP)
        qa = jnp.concatenate([c for hp in range(GROUP // 2)
                              for c in _placed_queries(qts, hp, par)], axis=1)
        st_scr[...] = jnp.dot(kc, qa, preferred_element_type=F32)
        ss = [jnp.where(valid, st_scr[:, h * qb:(h + 1) * qb], NEG) for h in heads]
        ms = [jnp.max(s, axis=0, keepdims=True) for s in ss]
        ms = [jnp.where(m > NEG, m, 0.0) for m in ms]
        exs = [jnp.exp2(s - m) for s, m in zip(ss, ms)]
        invs = [1.0 / jnp.maximum(jnp.sum(ex, axis=0, keepdims=True), 1e-30) for ex in exs]
        ps = [ex * inv for ex, inv in zip(exs, invs)]
        p_sum = functools.reduce(lambda a, b: a + b, ps)
        p_all = jnp.concatenate([p.astype(BF16) for p in ps], axis=1)
        o_all = jnp.dot(vt, p_all, preferred_element_type=F32)
        for hp in range(GROUP // 2):
            ot_scr[par, hp] = o_all[:, hp * ch:(hp + 1) * ch]

        imp_t = jnp.zeros((n_sel, qb), F32)
        rest = p_sum
        for _ in range(3):
            piece = rest.astype(BF16)
            rest = rest - piece.astype(F32)
            imp_t = imp_t + jnp.dot(ovt_ref[...], piece, preferred_element_type=F32)
        forced = (blk == 0) | (blk == cur) | (blk == cur - 1)
        imp_t = jnp.where(forced, jnp.inf, imp_t)
        imp_t = jnp.where(blk > cur, -jnp.inf, imp_t)
        groups = [imp_t[g * F32_ROWS:(g + 1) * F32_ROWS] for g in range(n_sel // F32_ROWS)]
        ranks = [jnp.zeros((F32_ROWS, qb), jnp.int32) for _ in groups]
        blk_in_group = lax.broadcasted_iota(jnp.int32, (F32_ROWS, qb), 0)
        for r in range(n_sel):
            row = imp_t[r:r + 1, :]
            for g, x in enumerate(groups):
                if g > r // F32_ROWS:
                    before = row >= x
                elif g < r // F32_ROWS:
                    before = row > x
                else:
                    before = (row > x) | ((row == x) & (blk_in_group > r % F32_ROWS))
                ranks[g] = ranks[g] + before.astype(jnp.int32)
        rank = jnp.concatenate(ranks, axis=0)
        bias_t = jnp.where(rank < n_top, 0.0, NEG)
        mbt_ref[0, par] = jnp.concatenate(
            [bias_t, jnp.zeros((LANES - n_sel, qb), F32)], axis=0).astype(mbt_ref.dtype)

    _store_token_major(o_ref, lambda par, hp: ot_scr[par, hp] * _gate_row(gt_ref, par, hp), qb)


def _cmp_select(proj, kcmp, vcmp_t, ovt, gt, batch, seq, n_cmp):
    qb = min(2 * ATT_QB, seq)
    nq = seq // qb
    nck = kcmp.shape[2]
    n_sel = seq // SLC_BLOCK
    assert n_sel <= LANES
    kern = functools.partial(_cmp_select_kernel, qb=qb, n_cmp=n_cmp, n_sel=n_sel,
                             n_top=min(SLC_TOPK, n_sel))
    return pl.pallas_call(
        kern,
        grid=(batch, 2, nq),
        in_specs=[pl.BlockSpec((qb, GROUP * LANES), lambda b, p, i: (b * nq + i, p)),
                  pl.BlockSpec((1, 1, nck, LANES), lambda b, p, i: (b, p, 0, 0)),
                  pl.BlockSpec((1, 1, LANES, nck), lambda b, p, i: (b, p, 0, 0)),
                  pl.BlockSpec((n_sel, nck), lambda b, p, i: (0, 0)),
                  pl.BlockSpec((1, 1, 1, 2 * GROUP, qb), lambda b, p, i: (b, 0, p, 0, i))],
        out_specs=[pl.BlockSpec((qb, GROUP * LANES), lambda b, p, i: (b * nq + i, p)),
                   pl.BlockSpec((1, 2, LANES, qb), lambda b, p, i: (b, p, 0, i))],
        out_shape=[jax.ShapeDtypeStruct((batch * seq, D_MODEL), BF16),
                   jax.ShapeDtypeStruct((batch, N_GROUPS, LANES, seq), BF16)],
        scratch_shapes=[pltpu.VMEM((2, GROUP // 2, HEAD_DIM, 2 * qb), F32),
                        pltpu.VMEM((nck, GROUP * qb), F32)],
        compiler_params=_params(("parallel", "parallel", "arbitrary")),
        name="cmp_select",
    )(proj, kcmp, vcmp_t, ovt, gt)


def _silu(z):
    return z / (1.0 + jnp.exp(-z))


def _out_proj_kernel(*refs, mode, tm, seq, n_z, final_norm):
    it = iter(refs)
    x_ref = next(it)
    if mode == "a":
        o_ref = next(it)
        z = jnp.concatenate([next(it)[...] for _ in range(n_z)], axis=1)
        a = o_ref[...].astype(F32) * _silu(z.astype(F32))
    elif mode == "b":
        oc_ref, os_ref, ow_ref = (next(it) for _ in range(3))
        z = jnp.concatenate([next(it)[...] for _ in range(n_z)], axis=1)
        o = oc_ref[...].astype(F32) + os_ref[...].astype(F32) + ow_ref[...].astype(F32)
        a = o * _silu(z.astype(F32))
    else:
        u_ref, bg_ref, c_ref, z_ref, uh_ref, ch_ref, cw_ref = (next(it) for _ in range(7))
        v = c_ref[...].astype(F32) * u_ref[...].astype(F32)
        first = (pl.program_id(0) * tm) % seq == 0
        vh = ch_ref[...].astype(F32) * uh_ref[...].astype(F32)
        vh = jnp.where(first, 0.0, vh)
        h1, h2 = vh[7:8, :], vh[6:7, :]
        row = lax.broadcasted_iota(jnp.int32, v.shape, 0)
        v1 = jnp.where(row == 0, h1, pltpu.roll(v, 1, 0))
        v2 = jnp.where(row == 0, h2, jnp.where(row == 1, h1, pltpu.roll(v, 2, 0)))
        y = cw_ref[0:1, :] * v2 + cw_ref[1:2, :] * v1 + cw_ref[2:3, :] * v
        a = bg_ref[...].astype(F32) * y * _silu(z_ref[...].astype(F32))
    w_ref = next(it)
    fw_ref = next(it) if final_norm else None
    out_ref = next(it)
    xn = x_ref[...] + jnp.dot(a.astype(BF16), w_ref[...], preferred_element_type=F32)
    if final_norm:
        ms = jnp.mean(xn * xn, axis=-1, keepdims=True)
        xn = xn * lax.rsqrt(ms + NORM_EPS) * fw_ref[...]
    out_ref[...] = xn


def _out_proj(x, w_out, mode, inputs, seq, final_w=None):
    T, D = x.shape
    tm = OUT_TM
    assert T % tm == 0 and seq % tm == 0
    row = lambda c: pl.BlockSpec((tm, D), lambda i, c=c: (i, c))
    in_specs, args = [row(0)], [x]
    n_z = D // PROJ_TN
    if mode in ("a", "b"):
        *outs, proj, z_col = inputs
        assert z_col % PROJ_TN == 0
        in_specs += [row(0)] * len(outs)
        in_specs += [pl.BlockSpec((tm, PROJ_TN), lambda i, c=z_col // PROJ_TN + n: (i, c))
                     for n in range(n_z)]
        args += outs + [proj] * n_z
    else:
        proj, conv_w = inputs
        halo = lambda c: pl.BlockSpec((8, D), lambda i, c=c: (jnp.maximum(i * (tm // 8) - 1, 0), c))
        in_specs += [row(0), row(1), row(2), row(3), halo(0), halo(2),
                     pl.BlockSpec(conv_w.shape, lambda i: (0, 0))]
        args += [proj, proj, proj, proj, proj, proj, conv_w]
    in_specs.append(pl.BlockSpec((D, D), lambda i: (0, 0)))
    args.append(w_out)
    if final_w is not None:
        in_specs.append(pl.BlockSpec((1, D), lambda i: (0, 0)))
        args.append(final_w.reshape(1, D))
    kern = functools.partial(_out_proj_kernel, mode=mode, tm=tm, seq=seq, n_z=n_z,
                             final_norm=final_w is not None)
    return pl.pallas_call(
        kern,
        grid=(T // tm,),
        in_specs=in_specs,
        out_specs=pl.BlockSpec((tm, D), lambda i: (i, 0)),
        out_shape=jax.ShapeDtypeStruct((T, D), F32),
        compiler_params=_params(("parallel",)),
        name="out_proj_" + mode,
    )(*args)


def _rope_tables(positions):
    inv_freq = ROPE_THETA ** (-jnp.arange(0, ROPE_DIM, 2, dtype=F32) / ROPE_DIM)
    head = jnp.concatenate([inv_freq, inv_freq, jnp.zeros((HEAD_DIM - ROPE_DIM,), F32)])
    ang = positions.astype(F32).reshape(-1, 1) * jnp.concatenate([head, head])[None, :]
    cos, sin = jnp.cos(ang), jnp.sin(ang)
    d = np.arange(LANES) % HEAD_DIM
    sin_a = jnp.where(jnp.asarray(d < ROPE_HALF), -sin, 0.0)
    sin_b = jnp.where(jnp.asarray((d >= ROPE_HALF) & (d < ROPE_DIM)), sin, 0.0)
    return cos, sin_a, sin_b


def _overlap_t(seq):
    nc = (seq - CMP_BLOCK) // CMP_STRIDE + 1
    nsel = seq // SLC_BLOCK
    c_start = np.arange(nc) * CMP_STRIDE
    c_end = c_start + CMP_BLOCK
    s_start = np.arange(nsel) * SLC_BLOCK
    s_end = s_start + SLC_BLOCK
    ov = np.clip(np.minimum(c_end[:, None], s_end[None, :]) - np.maximum(c_start[:, None], s_start[None, :]), 0, None)
    ov = (ov / CMP_BLOCK).astype(np.float32)
    ovt = np.zeros((nsel, seq // CMP_STRIDE), np.float32)
    ovt[:, :nc] = ov.T
    return jnp.asarray(ovt, BF16), nc


def _values_t(proj, col, batch, seq, kb):
    v = proj[:, col:col + 2 * LANES].reshape(batch, seq // kb, kb, 2, LANES)
    return v.transpose(0, 3, 1, 4, 2)


def _mixer_a(x, norm_g, w_in, layer, sinks, w_out, tabs, batch, seq, final_w=None):
    proj = _norm_proj(x, norm_g, w_in, layer, tabs, query_tiles=(0, 1, 2, 3), rope_half_tiles=(4,))
    o = _swa(proj, _values_t(proj, 2304, batch, seq, ATT_QB), sinks, batch, seq, k_col=16,
             window=A_WINDOW)
    return _out_proj(x, w_out.astype(BF16), "a", (o, proj, 2560), seq, final_w)


def _mixer_b(x, norm_g, w_in, kc_pos, kc_w1, kc_w2, vc_pos, vc_w1, vc_w2, w_out, tabs, batch, seq,
             final_w=None):
    c = [2048 + 256 * n for n in range(7)]
    q = w_in[:, :2048]
    kc, vc, ks, vs, kw, vw = (w_in[:, c[n]:c[n + 1]] for n in range(6))
    gates = w_in[:, c[6]:c[6] + 96]
    z = w_in[:, c[6] + 96:]
    w = jnp.concatenate([q, z, ks, kw, kc, vc, vs, vw, gates,
                         jnp.zeros((D_MODEL, PROJ_TN - 96), F32)], axis=1).astype(BF16)
    proj = _norm_proj(x, norm_g, w[None], 0, tabs, query_tiles=(0, 1, 2, 3), rope_full_tiles=(8,))

    chunks = seq // CMP_STRIDE

    def chunked(col):
        t = proj[:, col:col + 256].reshape(batch, chunks, CMP_STRIDE, N_GROUPS, HEAD_DIM)
        return t.transpose(0, 3, 1, 2, 4).reshape(batch, N_GROUPS, chunks, CMP_STRIDE * HEAD_DIM)

    def w2_pair(w2, transposed):
        zero = jnp.zeros_like(w2)
        pair = jnp.stack([jnp.concatenate([w2, zero], axis=1), jnp.concatenate([zero, w2], axis=1)])
        return (pair.transpose(0, 2, 1) if transposed else pair).astype(BF16)

    half = CMP_STRIDE * HEAD_DIM
    kcmp = _compress(chunked(4608), kc_pos.reshape(2, half), kc_w1.astype(BF16),
                     w2_pair(kc_w2, False), False)
    vcmp_t = _compress(chunked(4864), vc_pos.reshape(2, half), vc_w1.astype(BF16),
                       w2_pair(vc_w2, True), True)

    ovt, n_cmp = _overlap_t(seq)
    gt = proj[:, 5632:5632 + 3 * N_HEADS].reshape(batch, seq, 3, 2, 2 * GROUP).transpose(0, 2, 3, 4, 1)
    o_cmp, mbt = _cmp_select(proj, kcmp, vcmp_t, ovt, gt, batch, seq, n_cmp)
    e_all = jnp.asarray((np.arange(seq)[:, None] // SLC_BLOCK) == np.arange(LANES)[None, :], BF16)
    kb = 256
    o_slc = _flash(proj, _values_t(proj, 5120, batch, seq, kb), batch, seq, k_col=32,
                   window=seq, kb=kb, mbt=mbt, e_all=e_all, gate=(gt, 1))
    o_win = _flash(proj, _values_t(proj, 5376, batch, seq, kb), batch, seq, k_col=34,
                   window=B_WINDOW, kb=kb, gate=(gt, 2))

    return _out_proj(x, w_out.astype(BF16), "b", (o_cmp, o_slc, o_win, proj, 2048), seq, final_w)


def _mixer_c(x, norm_g, w_in, layer, conv_w, w_out, tabs, seq, final_w=None):
    proj = _norm_proj(x, norm_g, w_in, layer, tabs)
    cw = jnp.concatenate([conv_w, jnp.zeros((8 - conv_w.shape[0], conv_w.shape[1]), F32)], axis=0)
    return _out_proj(x, w_out.astype(BF16), "c", (proj, cw), seq, final_w)


def kernel(x, positions, norm_w, final_norm_w, a_w_in, a_sinks, a_w_out, b_w_in, b_cmp_k_pos, b_cmp_k_w1,
           b_cmp_k_w2, b_cmp_v_pos, b_cmp_v_w1, b_cmp_v_w2, b_w_out, c_w_in, c_conv_w, c_w_out):
    batch, seq, d = x.shape
    depth = norm_w.shape[0]
    tabs = _rope_tables(positions)
    xf = x.reshape(batch * seq, d)
    for i in range(depth):
        kind, j = i % 3, i // 3
        fw = final_norm_w if i == depth - 1 else None
        if kind == 0:
            xf = _mixer_a(xf, norm_w[i], a_w_in, j, a_sinks[j], a_w_out[j], tabs, batch, seq, fw)
        elif kind == 1:
            xf = _mixer_b(xf, norm_w[i], b_w_in[j], b_cmp_k_pos[j], b_cmp_k_w1[j], b_cmp_k_w2[j],
                          b_cmp_v_pos[j], b_cmp_v_w1[j], b_cmp_v_w2[j], b_w_out[j], tabs, batch, seq, fw)
        else:
            xf = _mixer_c(xf, norm_w[i], c_w_in, j, c_conv_w[j], c_w_out[j], tabs, seq, fw)
    return xf.reshape(batch, seq, d)
```
